```python
import math, functools
import jax, jax.numpy as jnp
from jax import lax
import numpy as np

D_MODEL = 1024
BATCH = 4
SEQ = 8192
DEPTH = 1
DEC_BATCH = 32
DEC_SEQ = 8
PAST_LEN = 16384
PAGE_SIZE = 128

MIX_WIDTH = D_MODEL
HGRN_WIDTH = MIX_WIDTH // 2
HGRN_HEAD_DIM = 128
HGRN_HEADS = HGRN_WIDTH // HGRN_HEAD_DIM
HGRN_CHUNK = 64
DIFF_WIDTH = MIX_WIDTH - HGRN_WIDTH
DIFF_HEAD_DIM = 64
DIFF_V_DIM = 2 * DIFF_HEAD_DIM
DIFF_HEADS = DIFF_WIDTH // DIFF_V_DIM
ROT_DIM = DIFF_HEAD_DIM // 4
ROPE_THETA = 500000.0
Q_BLOCK = 128
N_EXPERTS = 32
TOP_K = 4
D_FF = D_MODEL
SWIGLU_LIMIT = 7.0
SWIGLU_ALPHA = 1.702
MOE_BLOCK = 128
NORM_EPS = 1e-6
IN_COLS = 4 * HGRN_WIDTH + 3 * DIFF_WIDTH

kernel_name = 'hybrid_hgrn2_diffattn_moe_step'


def _rmsnorm(x, w):
    xf = x.astype(jnp.float32)
    y = xf * lax.rsqrt(jnp.mean(xf * xf, axis=-1, keepdims=True) + NORM_EPS)
    return (y * w.astype(jnp.float32)).astype(x.dtype)


def _partial_rope(x, pos):
    half = ROT_DIM // 2
    inv = ROPE_THETA ** (-jnp.arange(0, ROT_DIM, 2, dtype=jnp.float32) / ROT_DIM)
    ang = pos.astype(jnp.float32)[:, None] * inv[None, :]
    cos = jnp.cos(ang).reshape(pos.shape[0], 1, 1, half)
    sin = jnp.sin(ang).reshape(pos.shape[0], 1, 1, half)
    xr = x[..., :ROT_DIM].astype(jnp.float32)
    x1, x2 = xr[..., :half], xr[..., half:]
    rot = jnp.concatenate([x1 * cos - x2 * sin, x2 * cos + x1 * sin], axis=-1)
    return jnp.concatenate([rot.astype(x.dtype), x[..., ROT_DIM:]], axis=-1)


def _hgrn2_chunked(q, k, v, log_f, s0):
    B, L, H, K = q.shape
    V = v.shape[-1]
    C = min(HGRN_CHUNK, L)
    n = L // C

    def blk(t):
        return t.reshape(B, n, C, H, t.shape[-1]).transpose(1, 0, 3, 2, 4)

    q, k, v, log_f = blk(q), blk(k), blk(v), blk(log_f)
    b = jnp.cumsum(log_f, axis=3)
    b_last = b[:, :, :, -1:, :]
    qe = q * jnp.exp(b)
    kd = k * jnp.exp(-b)
    causal = jnp.tril(jnp.ones((C, C), dtype=bool))
    a = jnp.where(causal, jnp.einsum('nbhck,nbhdk->nbhcd', qe, kd), 0.0)
    o_intra = jnp.einsum('nbhcd,nbhdv->nbhcv', a, v)
    ds = jnp.einsum('nbhck,nbhcv->nbhkv', k * jnp.exp(b_last - b), v)
    decay = jnp.exp(b_last[:, :, :, 0, :])

    def step(s, inp):
        qe_c, ds_c, dec_c = inp
        o_c = jnp.einsum('bhck,bhkv->bhcv', qe_c, s)
        return dec_c[..., None] * s + ds_c, o_c

    s_fin, o_inter = lax.scan(step, s0, (qe, ds, decay))
    o = (o_intra + o_inter).transpose(1, 0, 3, 2, 4).reshape(B, L, H, V)
    return o, s_fin


def _causal_diff_attention(q, k, v, lam):
    B, S, H, _, d = q.shape
    bq = min(Q_BLOCK, S)
    nb = S // bq
    scale = DIFF_HEAD_DIM ** -0.5
    qb = q.reshape(B, nb, bq, H, 2, d).swapaxes(0, 1)
    kpos = jnp.arange(S)

    def block(args):
        qi, i = args
        s = jnp.einsum('bqhcd,bkhcd->bhcqk', qi, k, preferred_element_type=jnp.float32) * scale
        qpos = i * bq + jnp.arange(bq)
        mask = kpos[None, :] <= qpos[:, None]
        p = jax.nn.softmax(jnp.where(mask, s, -jnp.inf), axis=-1)
        a = (p[:, :, 0] - lam * p[:, :, 1]).astype(v.dtype)
        return jnp.einsum('bhqk,bkhe->bqhe', a, v)

    o = lax.map(block, (qb, jnp.arange(nb)))
    return o.swapaxes(0, 1).reshape(B, S, H, v.shape[-1])


def _paged_diff_attention(q, k_new, v_new, lam, cache_k, cache_v, layer, page_table):
    db, t = q.shape[0], q.shape[1]
    kp = cache_k[layer, page_table].reshape(db, -1, DIFF_HEADS, 2, DIFF_HEAD_DIM)
    vp = cache_v[layer, page_table].reshape(db, -1, DIFF_HEADS, DIFF_V_DIM)
    n_past = kp.shape[1]
    scale = DIFF_HEAD_DIM ** -0.5
    s_past = jnp.einsum('bqhcd,bkhcd->bhcqk', q, kp, preferred_element_type=jnp.float32) * scale
    s_new = jnp.einsum('bqhcd,bkhcd->bhcqk', q, k_new, preferred_element_type=jnp.float32) * scale
    causal = jnp.tril(jnp.ones((t, t), dtype=bool))
    s_new = jnp.where(causal, s_new, -jnp.inf)
    p = jax.nn.softmax(jnp.concatenate([s_past, s_new], axis=-1), axis=-1)
    a = (p[:, :, 0] - lam * p[:, :, 1]).astype(v_new.dtype)
    return (jnp.einsum('bhqk,bkhe->bqhe', a[..., :n_past], vp)
            + jnp.einsum('bhqk,bkhe->bqhe', a[..., n_past:], v_new))


def _moe(x2d, w_router, b_router, w_gate, b_gate, w_up, b_up, w_down, b_down):
    T, D = x2d.shape
    logits = jnp.dot(x2d, w_router, preferred_element_type=jnp.float32) + b_router.astype(jnp.float32)
    top_val, top_idx = lax.top_k(logits, TOP_K)
    gate = jax.nn.softmax(top_val, axis=-1)
    n_assign = T * TOP_K
    flat_e = top_idx.reshape(-1).astype(jnp.int32)
    order = jnp.argsort(flat_e).astype(jnp.int32)
    sorted_e = flat_e[order]
    sorted_tok = order // TOP_K
    sorted_gate = gate.reshape(-1)[order]
    counts = jnp.bincount(flat_e, length=N_EXPERTS).astype(jnp.int32)
    padded = (counts + MOE_BLOCK - 1) // MOE_BLOCK * MOE_BLOCK
    start_sorted = jnp.cumsum(counts) - counts
    pad_end = jnp.cumsum(padded)
    start_pad = pad_end - padded
    dest = start_pad[sorted_e] + jnp.arange(n_assign, dtype=jnp.int32) - start_sorted[sorted_e]
    n_rows = (n_assign + N_EXPERTS * (MOE_BLOCK - 1) + MOE_BLOCK - 1) // MOE_BLOCK * MOE_BLOCK
    n_blocks = n_rows // MOE_BLOCK
    row_tok = jnp.zeros((n_rows,), jnp.int32).at[dest].set(sorted_tok)
    block_e = jnp.minimum(jnp.searchsorted(pad_end, jnp.arange(n_blocks, dtype=jnp.int32) * MOE_BLOCK, side='right'), N_EXPERTS - 1)
    xb = x2d[row_tok].reshape(n_blocks, MOE_BLOCK, D)

    def expert_block(args):
        xblk, e = args
        g = xblk @ w_gate[e] + b_gate[e]
        u = xblk @ w_up[e] + b_up[e]
        g = jnp.minimum(g, SWIGLU_LIMIT)
        u = jnp.clip(u, -SWIGLU_LIMIT, SWIGLU_LIMIT)
        h = (u + 1.0) * (g * jax.nn.sigmoid(SWIGLU_ALPHA * g))
        return h @ w_down[e] + b_down[e]

    yb = lax.map(expert_block, (xb, block_e)).reshape(n_rows, D)
    y_assign = yb[dest] * sorted_gate[:, None].astype(yb.dtype)
    return jax.ops.segment_sum(y_assign, sorted_tok, num_segments=T)


def _layer(x, pos, s0, attend, norm_mix_w, w_in, lb, hgrn_norm_w, lam, lam_init, subln_w, w_out, norm_ffn_w, moe_w):
    f32 = jnp.float32
    B, L, D = x.shape
    h = _rmsnorm(x, norm_mix_w)
    proj = h @ w_in
    cuts = [HGRN_WIDTH, 2 * HGRN_WIDTH, 3 * HGRN_WIDTH, 4 * HGRN_WIDTH,
            4 * HGRN_WIDTH + DIFF_WIDTH, 4 * HGRN_WIDTH + 2 * DIFF_WIDTH]
    hq, hf, hi, hg, dq, dk, dv = jnp.split(proj, cuts, axis=-1)
    hshape = (B, L, HGRN_HEADS, HGRN_HEAD_DIM)
    lbh = lb.reshape(HGRN_HEADS, HGRN_HEAD_DIM)
    q_h = jax.nn.silu(hq.astype(f32)).reshape(hshape)
    f = lbh + (1.0 - lbh) * jax.nn.sigmoid(hf.astype(f32).reshape(hshape))
    o_h, s_fin = _hgrn2_chunked(q_h, 1.0 - f, hi.astype(f32).reshape(hshape), jnp.log(f), s0.astype(f32))
    o_h = _rmsnorm(o_h, hgrn_norm_w.reshape(HGRN_HEADS, HGRN_HEAD_DIM)) * jax.nn.silu(hg.astype(f32).reshape(hshape))
    o_h = o_h.reshape(B, L, HGRN_WIDTH).astype(x.dtype)
    q_d = _partial_rope(dq.reshape(B, L, DIFF_HEADS, 2, DIFF_HEAD_DIM), pos)
    k_d = _partial_rope(dk.reshape(B, L, DIFF_HEADS, 2, DIFF_HEAD_DIM), pos)
    v_d = dv.reshape(B, L, DIFF_HEADS, DIFF_V_DIM)
    o_d = attend(q_d, k_d, v_d, lam)
    o_d = (_rmsnorm(o_d, subln_w) * (1.0 - lam_init)).reshape(B, L, DIFF_WIDTH).astype(x.dtype)
    x = x + jnp.concatenate([o_h, o_d], axis=-1) @ w_out
    ff = _moe(_rmsnorm(x, norm_ffn_w).reshape(B * L, D), *moe_w).reshape(B, L, D)
    y = x + ff.astype(x.dtype)
    k_rows = k_d.reshape(B, L, DIFF_HEADS, 2 * DIFF_HEAD_DIM)
    return y, k_rows, v_d, s_fin.astype(x.dtype)


def setup_inputs(seed: int = 0) -> dict:
    key = jax.random.key(seed)
    ks = jax.random.split(key, 32)
    n_pages = PAST_LEN // PAGE_SIZE
    n_used = DEC_BATCH * n_pages
    n_pool = n_used + n_used // 4

    def nrm(k, shape, s):
        return jax.random.normal(k, shape, jnp.float32) * s

    def gain(k, shape):
        return 1.0 + nrm(k, shape, 0.02)

    page_table = jax.random.permutation(ks[5], n_pool)[:n_used].reshape(DEC_BATCH, n_pages).astype(jnp.int32)
    return {
        'x_prompt': nrm(ks[0], (BATCH, SEQ, D_MODEL), 1.0),
        'x_sample': nrm(ks[1], (DEC_BATCH, DEC_SEQ, D_MODEL), 1.0),
        'cache_k': nrm(ks[2], (DEPTH, n_pool, PAGE_SIZE, DIFF_HEADS, 2 * DIFF_HEAD_DIM), 1.0),
        'cache_v': nrm(ks[3], (DEPTH, n_pool, PAGE_SIZE, DIFF_HEADS, 2 * DIFF_HEAD_DIM), 1.0),
        'state_hgrn': nrm(ks[4], (DEPTH, DEC_BATCH, HGRN_HEADS, HGRN_HEAD_DIM, HGRN_HEAD_DIM), 0.5),
        'page_table': page_table,
        'w_norm_mix': gain(ks[6], (DEPTH, D_MODEL)),
        'w_in': nrm(ks[7], (DEPTH, D_MODEL, IN_COLS), D_MODEL ** -0.5),
        'hgrn_lb_logits': nrm(ks[8], (DEPTH + 1, HGRN_WIDTH), 0.1),
        'w_hgrn_norm': gain(ks[9], (DEPTH, HGRN_WIDTH)),
        'diff_lambda_q1': nrm(ks[10], (DEPTH, DIFF_HEAD_DIM), 0.1),
        'diff_lambda_k1': nrm(ks[11], (DEPTH, DIFF_HEAD_DIM), 0.1),
        'diff_lambda_q2': nrm(ks[12], (DEPTH, DIFF_HEAD_DIM), 0.1),
        'diff_lambda_k2': nrm(ks[13], (DEPTH, DIFF_HEAD_DIM), 0.1),
        'w_subln': gain(ks[14], (DEPTH, DIFF_V_DIM)),
        'w_out': nrm(ks[15], (DEPTH, MIX_WIDTH, D_MODEL), MIX_WIDTH ** -0.5),
        'w_norm_ffn': gain(ks[16], (DEPTH, D_MODEL)),
        'w_router': nrm(ks[17], (DEPTH, D_MODEL, N_EXPERTS), D_MODEL ** -0.5),
        'b_router': nrm(ks[18], (DEPTH, N_EXPERTS), 0.01),
        'w_gate': nrm(ks[19], (DEPTH, N_EXPERTS, D_MODEL, D_FF), D_MODEL ** -0.5),
        'b_gate': nrm(ks[20], (DEPTH, N_EXPERTS, D_FF), 0.01),
        'w_up': nrm(ks[21], (DEPTH, N_EXPERTS, D_MODEL, D_FF), D_MODEL ** -0.5),
        'b_up': nrm(ks[22], (DEPTH, N_EXPERTS, D_FF), 0.01),
        'w_down': nrm(ks[23], (DEPTH, N_EXPERTS, D_FF, D_MODEL), D_FF ** -0.5),
        'b_down': nrm(ks[24], (DEPTH, N_EXPERTS, D_MODEL), 0.01),
        'w_norm_final': gain(ks[25], (D_MODEL,)),
    }


def reference(x_prompt, x_sample, cache_k, cache_v, state_hgrn, page_table, w_norm_mix, w_in, hgrn_lb_logits, w_hgrn_norm, diff_lambda_q1, diff_lambda_k1, diff_lambda_q2, diff_lambda_k2, w_subln, w_out, w_norm_ffn, w_router, b_router, w_gate, b_gate, w_up, b_up, w_down, b_down, w_norm_final):
    f32 = jnp.float32
    seq = x_prompt.shape[1]
    dec_seq = x_sample.shape[1]
    past_len = page_table.shape[1] * cache_k.shape[2]
    pos_prompt = jnp.arange(seq, dtype=jnp.int32)
    pos_sample = past_len + jnp.arange(dec_seq, dtype=jnp.int32)
    lower_bounds = jnp.cumsum(jax.nn.softmax(hgrn_lb_logits.astype(f32), axis=0), axis=0)
    s0_prompt = jnp.zeros((x_prompt.shape[0], HGRN_HEADS, HGRN_HEAD_DIM, HGRN_HEAD_DIM), x_prompt.dtype)
    hp, hs = x_prompt, x_sample
    k_p, v_p, s_p, k_s, v_s, s_s = [], [], [], [], [], []
    for l in range(DEPTH):
        lam_init = 0.8 - 0.6 * math.exp(-0.3 * l)
        lam = (jnp.exp(jnp.sum(diff_lambda_q1[l].astype(f32) * diff_lambda_k1[l].astype(f32)))
               - jnp.exp(jnp.sum(diff_lambda_q2[l].astype(f32) * diff_lambda_k2[l].astype(f32))) + lam_init)
        moe_w = (w_router[l], b_router[l], w_gate[l], b_gate[l], w_up[l], b_up[l], w_down[l], b_down[l])
        shared = (w_norm_mix[l], w_in[l], lower_bounds[l], w_hgrn_norm[l], lam, lam_init, w_subln[l], w_out[l], w_norm_ffn[l], moe_w)
        hp, kp_, vp_, sp_ = _layer(hp, pos_prompt, s0_prompt, _causal_diff_attention, *shared)
        attend_sample = functools.partial(_paged_diff_attention, cache_k=cache_k, cache_v=cache_v, layer=l, page_table=page_table)
        hs, ks_, vs_, ss_ = _layer(hs, pos_sample, state_hgrn[l], attend_sample, *shared)
        k_p.append(kp_); v_p.append(vp_); s_p.append(sp_)
        k_s.append(ks_); v_s.append(vs_); s_s.append(ss_)
    y_prompt = _rmsnorm(hp, w_norm_final)
    y_sample = _rmsnorm(hs, w_norm_final)
    return (y_prompt, y_sample, jnp.stack(k_p), jnp.stack(v_p), jnp.stack(s_p), jnp.stack(k_s), jnp.stack(v_s), jnp.stack(s_s))
```

```python
import functools
import math

import jax
import jax.numpy as jnp
from jax import lax
from jax.experimental import pallas as pl
from jax.experimental.pallas import tpu as pltpu

F32 = jnp.float32
BF16 = jnp.bfloat16

LANES = 128
HGRN_HEAD_DIM = 128
HGRN_CHUNK = 64
DIFF_HEAD_DIM = 64
DIFF_V_DIM = 2 * DIFF_HEAD_DIM
ROT_DIM = DIFF_HEAD_DIM // 4
ROPE_THETA = 500000.0
TOP_K = 4
SWIGLU_LIMIT = 7.0
SWIGLU_ALPHA = 1.702
NORM_EPS = 1e-6
VMEM_LIMIT = 56 * 1024 * 1024

ROW_TILE = 512
ATTN_TILE = 512
HGRN_TILE = 512
MOE_TILE = 256
PAGES_PER_STEP = 8
COMBINE_TILE = 256

_NT = (((1,), (1,)), ((), ()))


def _params(n_axes):
    return pltpu.CompilerParams(dimension_semantics=("arbitrary",) * n_axes, vmem_limit_bytes=VMEM_LIMIT)


def _rms(x, w):
    return x * lax.rsqrt(jnp.mean(x * x, axis=-1, keepdims=True) + NORM_EPS) * w


def _in_proj_kernel(x_ref, wn_ref, w_ref, lb_ref, rc_ref, rp_ref, rm_ref,
                    qh_ref, f_ref, vi_ref, g_ref, qd_ref, kr_ref, kb_ref, vr_ref, vb_ref, *, hw, dw):
    h = _rms(x_ref[...], wn_ref[...]).astype(BF16)

    def proj(c0, width):
        return jnp.dot(h, w_ref[:, c0:c0 + width], preferred_element_type=F32)

    hq = proj(0, hw)
    qh_ref[...] = hq * jax.nn.sigmoid(hq)
    lb = lb_ref[...]
    f_ref[...] = lb + (1.0 - lb) * jax.nn.sigmoid(proj(hw, hw))
    vi_ref[...] = proj(2 * hw, hw)
    hg = proj(3 * hw, hw)
    g_ref[...] = hg * jax.nn.sigmoid(hg)

    rc, rp, rm = rc_ref[...], rp_ref[...], rm_ref[...]
    half = ROT_DIM // 2

    def rope(t):
        return t * rc + pltpu.roll(t, half, 1) * rp + pltpu.roll(t, LANES - half, 1) * rm

    dq = proj(4 * hw, dw)
    dk = proj(4 * hw + dw, dw)
    dv = proj(4 * hw + 2 * dw, dw)
    scale = DIFF_HEAD_DIM ** -0.5
    for c in range(dw // LANES):
        sl = slice(c * LANES, (c + 1) * LANES)
        qd_ref[:, sl] = (rope(dq[:, sl]) * scale).astype(qd_ref.dtype)
        kr = rope(dk[:, sl])
        kr_ref[:, sl] = kr
        kb_ref[:, sl] = kr.astype(kb_ref.dtype)
    vr_ref[...] = dv
    vb_ref[...] = dv.astype(vb_ref.dtype)


def _in_proj(x2d, wn, w_bf, lb, tabs, tab_tiles, act_dtype, hw, dw):
    t, d = x2d.shape
    tm = min(ROW_TILE, t)
    row = lambda i: (i, 0)
    fixed = lambda i: (0, 0)
    tab = lambda i: (i % tab_tiles, 0)
    out_shapes = ([jax.ShapeDtypeStruct((t, hw), F32)] * 4
                  + [jax.ShapeDtypeStruct((t, dw), act_dtype), jax.ShapeDtypeStruct((t, dw), F32),
                     jax.ShapeDtypeStruct((t, dw), act_dtype), jax.ShapeDtypeStruct((t, dw), F32),
                     jax.ShapeDtypeStruct((t, dw), act_dtype)])
    return pl.pallas_call(
        functools.partial(_in_proj_kernel, hw=hw, dw=dw),
        grid=(t // tm,),
        in_specs=[pl.BlockSpec((tm, d), row),
                  pl.BlockSpec((1, d), fixed),
                  pl.BlockSpec(w_bf.shape, fixed, pipeline_mode=pl.Buffered(1)),
                  pl.BlockSpec((1, hw), fixed),
                  pl.BlockSpec((tm, LANES), tab), pl.BlockSpec((tm, LANES), tab), pl.BlockSpec((tm, LANES), tab)],
        out_specs=[pl.BlockSpec((tm, hw), row)] * 4 + [pl.BlockSpec((tm, dw), row)] * 5,
        out_shape=out_shapes,
        compiler_params=_params(1),
        name="in_proj",
    )(x2d, wn.reshape(1, d), w_bf, lb.reshape(1, hw), *tabs)


def _rope_tables(pos):
    half = ROT_DIM // 2
    inv = ROPE_THETA ** (-jnp.arange(0, ROT_DIM, 2, dtype=F32) / ROT_DIM)
    ang = pos.astype(F32)[:, None] * inv[None, :]
    cos, sin = jnp.cos(ang), jnp.sin(ang)
    n = pos.shape[0]
    rest = DIFF_HEAD_DIM - ROT_DIM
    zeros_h = jnp.zeros((n, half), F32)
    rc = jnp.concatenate([cos, cos, jnp.ones((n, rest), F32)], axis=1)
    rp = jnp.concatenate([zeros_h, sin, jnp.zeros((n, rest), F32)], axis=1)
    rm = jnp.concatenate([-sin, zeros_h, jnp.zeros((n, rest), F32)], axis=1)
    reps = LANES // DIFF_HEAD_DIM
    return tuple(jnp.tile(a, (1, reps)) for a in (rc, rp, rm))


def _hgrn_kernel(*refs, has_state):
    if has_state:
        q_ref, f_ref, v_ref, g_ref, wn_ref, s0_ref, o_ref, sfin_ref, st_ref = refs
    else:
        q_ref, f_ref, v_ref, g_ref, wn_ref, o_ref, sfin_ref, st_ref = refs
    j = pl.program_id(2)
    c_len = HGRN_CHUNK
    kdim = HGRN_HEAD_DIM

    @pl.when(j == 0)
    def _():
        st_ref[...] = s0_ref[0, 0].T if has_state else jnp.zeros(st_ref.shape, F32)

    q, f, v = q_ref[...], f_ref[...], v_ref[...]
    rows = q.shape[0]
    if rows < c_len:
        pad = c_len - rows
        q = jnp.concatenate([q, jnp.zeros((pad, kdim), F32)], axis=0)
        v = jnp.concatenate([v, jnp.zeros((pad, kdim), F32)], axis=0)
        f = jnp.concatenate([f, jnp.ones((pad, kdim), F32)], axis=0)
    tl = q.shape[0]
    n_c = tl // c_len

    k = 1.0 - f
    b = jnp.log(f)
    r = lax.broadcasted_iota(jnp.int32, (tl, kdim), 0) % c_len
    s = 1
    while s < c_len:
        b = b + jnp.where(r >= s, pltpu.roll(b, s, 0), 0.0)
        s *= 2
    b3 = b.reshape(n_c, c_len, kdim)
    q3, k3, v3 = (a.reshape(n_c, c_len, kdim) for a in (q, k, v))
    bl = b3[:, c_len - 1:c_len, :]
    qe = (q3 * jnp.exp(b3)).astype(BF16)
    kd = (k3 * jnp.exp(-b3)).astype(BF16)
    kl = (k3 * jnp.exp(bl - b3)).astype(BF16)
    vb = v3.astype(BF16)
    dec = jnp.exp(bl)

    a = jnp.einsum('nck,ndk->ncd', qe, kd, preferred_element_type=F32)
    causal = (lax.broadcasted_iota(jnp.int32, (c_len, c_len), 1)
              <= lax.broadcasted_iota(jnp.int32, (c_len, c_len), 0))
    a = jnp.where(causal[None], a, 0.0).astype(BF16)
    o_intra = jnp.einsum('ncd,ndv->ncv', a, vb, preferred_element_type=F32)
    ds_t = jnp.einsum('ncv,nck->nvk', vb, kl, preferred_element_type=F32)

    st = st_ref[...]
    outs = []
    for c in range(n_c):
        o_inter = lax.dot_general(qe[c], st.astype(BF16), _NT, preferred_element_type=F32)
        outs.append(o_intra[c] + o_inter)
        st = st * dec[c] + ds_t[c]
    st_ref[...] = st
    o = outs[0] if n_c == 1 else jnp.concatenate(outs, axis=0)
    o = o[:rows]
    o_ref[...] = (_rms(o, wn_ref[...]) * g_ref[...]).astype(o_ref.dtype)

    @pl.when(j == pl.num_programs(2) - 1)
    def _():
        sfin_ref[0, 0] = st.T


def _hgrn(qh, f, vi, g, wn, s0, nb, seq, out_dtype):
    t, hw = qh.shape
    heads = hw // HGRN_HEAD_DIM
    tl = min(HGRN_TILE, seq)
    nl = seq // tl
    blk = pl.BlockSpec((tl, HGRN_HEAD_DIM), lambda b, h, j: (b * nl + j, h))
    state_blk = pl.BlockSpec((1, 1, HGRN_HEAD_DIM, HGRN_HEAD_DIM), lambda b, h, j: (b, h, 0, 0))
    in_specs = [blk, blk, blk, blk, pl.BlockSpec((1, HGRN_HEAD_DIM), lambda b, h, j: (0, h))]
    args = [qh, f, vi, g, wn.reshape(1, hw)]
    if s0 is not None:
        in_specs.append(state_blk)
        args.append(s0)
    return pl.pallas_call(
        functools.partial(_hgrn_kernel, has_state=s0 is not None),
        grid=(nb, heads, nl),
        in_specs=in_specs,
        out_specs=[blk, state_blk],
        out_shape=[jax.ShapeDtypeStruct((t, hw), out_dtype),
                   jax.ShapeDtypeStruct((nb, heads, HGRN_HEAD_DIM, HGRN_HEAD_DIM), F32)],
        scratch_shapes=[pltpu.VMEM((HGRN_HEAD_DIM, HGRN_HEAD_DIM), F32)],
        compiler_params=_params(3),
        name="hgrn",
    )(*args)


def _split_components(q):
    lane = lax.broadcasted_iota(jnp.int32, q.shape, 1)
    zero = jnp.zeros_like(q)
    return jnp.where(lane < DIFF_HEAD_DIM, q, zero), jnp.where(lane >= DIFF_HEAD_DIM, q, zero)


def _online_update(s, vb, m, l, acc):
    m_new = jnp.maximum(m, jnp.max(s, axis=-1, keepdims=True))
    alpha = jnp.exp(m - m_new)
    p = jnp.exp(s - m_new)
    l = alpha * l + jnp.sum(p, axis=-1, keepdims=True)
    acc = alpha * acc + jnp.dot(p.astype(BF16), vb, preferred_element_type=F32)
    return m_new, l, acc


def _diff_finish(o0, o1, sc_ref, wsub_ref):
    o = o0 - sc_ref[0] * o1
    return _rms(o, wsub_ref[...]) * sc_ref[1]


def _causal_attn_kernel(sc_ref, q_ref, k_ref, v_ref, wsub_ref, o_ref, *, tile):
    i = pl.program_id(2)
    q_lo, q_hi = _split_components(q_ref[...])

    def scores(kb):
        return (lax.dot_general(q_lo, kb, _NT, preferred_element_type=F32),
                lax.dot_general(q_hi, kb, _NT, preferred_element_type=F32))

    def body(j, carry):
        m0, l0, a0, m1, l1, a1 = carry
        start = pl.multiple_of(j * tile, tile)
        kb = k_ref[pl.ds(start, tile), :]
        vb = v_ref[pl.ds(start, tile), :]
        s0, s1 = scores(kb)
        return _online_update(s0, vb, m0, l0, a0) + _online_update(s1, vb, m1, l1, a1)

    neg = jnp.full((tile, 1), -jnp.inf, F32)
    zero1 = jnp.zeros((tile, 1), F32)
    zero_acc = jnp.zeros((tile, DIFF_V_DIM), F32)
    carry = lax.fori_loop(0, i, body, (neg, zero1, zero_acc, neg, zero1, zero_acc))

    start = pl.multiple_of(i * tile, tile)
    kb = k_ref[pl.ds(start, tile), :]
    vb = v_ref[pl.ds(start, tile), :]
    s0, s1 = scores(kb)
    keep = (lax.broadcasted_iota(jnp.int32, (tile, tile), 1) <= lax.broadcasted_iota(jnp.int32, (tile, tile), 0))
    s0 = jnp.where(keep, s0, -jnp.inf)
    s1 = jnp.where(keep, s1, -jnp.inf)
    m0, l0, a0, m1, l1, a1 = carry
    _, l0, a0 = _online_update(s0, vb, m0, l0, a0)
    _, l1, a1 = _online_update(s1, vb, m1, l1, a1)
    o_ref[...] = _diff_finish(a0 / l0, a1 / l1, sc_ref, wsub_ref).astype(o_ref.dtype)


def _causal_attn(qd, kb, vb, scal, wsub, nb, seq):
    t, dw = qd.shape
    heads = dw // DIFF_V_DIM
    tile = min(ATTN_TILE, seq)
    nq = seq // tile
    return pl.pallas_call(
        functools.partial(_causal_attn_kernel, tile=tile),
        grid=(nb, heads, nq),
        in_specs=[pl.BlockSpec(memory_space=pltpu.SMEM),
                  pl.BlockSpec((tile, DIFF_V_DIM), lambda b, h, i: (b * nq + i, h)),
                  pl.BlockSpec((seq, DIFF_V_DIM), lambda b, h, i: (b, h)),
                  pl.BlockSpec((seq, DIFF_V_DIM), lambda b, h, i: (b, h)),
                  pl.BlockSpec((1, DIFF_V_DIM), lambda b, h, i: (0, 0))],
        out_specs=pl.BlockSpec((tile, DIFF_V_DIM), lambda b, h, i: (b * nq + i, h)),
        out_shape=jax.ShapeDtypeStruct((t, dw), BF16),
        compiler_params=_params(3),
        name="causal_diff_attn",
    )(scal, qd, kb, vb, wsub.reshape(1, DIFF_V_DIM))


def _paged_attn_kernel(*refs, heads, dec_seq, n_pages):
    pt_ref, sc_ref, q_ref = refs[0], refs[1], refs[2]
    k_pages = refs[3:3 + n_pages]
    v_pages = refs[3 + n_pages:3 + 2 * n_pages]
    kn_ref, vn_ref, wsub_ref, o_ref, m_ref, l_ref, acc_ref = refs[3 + 2 * n_pages:]
    del pt_ref
    g = pl.program_id(1)
    rows = 2 * dec_seq

    @pl.when(g == 0)
    def _():
        m_ref[...] = jnp.full(m_ref.shape, -jnp.inf, F32)
        l_ref[...] = jnp.zeros(l_ref.shape, F32)
        acc_ref[...] = jnp.zeros(acc_ref.shape, F32)

    q = q_ref[...].astype(BF16)

    def q_blockdiag(h):
        lo, hi = _split_components(q[:, h * DIFF_V_DIM:(h + 1) * DIFF_V_DIM])
        return jnp.concatenate([lo, hi], axis=0)

    def accumulate(h, s, vb):
        m, l, acc = _online_update(s, vb, m_ref[h], l_ref[h], acc_ref[h])
        m_ref[h], l_ref[h], acc_ref[h] = m, l, acc

    for h in range(heads):
        qbd = q_blockdiag(h)
        sl = slice(h * DIFF_V_DIM, (h + 1) * DIFF_V_DIM)
        kb = jnp.concatenate([kp[0, :, sl].astype(BF16) for kp in k_pages], axis=0)
        vb = jnp.concatenate([vp[0, :, sl].astype(BF16) for vp in v_pages], axis=0)
        accumulate(h, lax.dot_general(qbd, kb, _NT, preferred_element_type=F32), vb)

    @pl.when(g == pl.num_programs(1) - 1)
    def _():
        page = kn_ref.shape[1]
        col = lax.broadcasted_iota(jnp.int32, (rows, page), 1)
        qpos = lax.broadcasted_iota(jnp.int32, (rows, page), 0) % dec_seq
        keep = col <= qpos
        for h in range(heads):
            sl = slice(h * DIFF_V_DIM, (h + 1) * DIFF_V_DIM)
            s = lax.dot_general(q_blockdiag(h), kn_ref[0, :, sl].astype(BF16), _NT, preferred_element_type=F32)
            accumulate(h, jnp.where(keep, s, -jnp.inf), vn_ref[0, :, sl].astype(BF16))
            o = acc_ref[h] / l_ref[h]
            o_ref[:, sl] = _diff_finish(o[:dec_seq], o[dec_seq:], sc_ref, wsub_ref).astype(o_ref.dtype)


def _paged_attn(qd, k_new, v_new, cache_k2, cache_v2, page_ids, scal, wsub, nb, dec_seq):
    t, dw = qd.shape
    heads = dw // DIFF_V_DIM
    page = cache_k2.shape[1]
    n_pages = page_ids.shape[1]
    gp = min(PAGES_PER_STEP, n_pages)
    n_steps = n_pages // gp
    pad = lambda a: jnp.pad(a.reshape(nb, dec_seq, dw), ((0, 0), (0, page - dec_seq), (0, 0)))
    page_spec = lambda i: pl.BlockSpec((1, page, dw), lambda b, g, pt: (pt[b, g * gp + i], 0, 0))
    tok_spec = pl.BlockSpec((dec_seq, dw), lambda b, g, pt: (b, 0))
    new_spec = pl.BlockSpec((1, page, dw), lambda b, g, pt: (b, 0, 0))
    grid_spec = pltpu.PrefetchScalarGridSpec(
        num_scalar_prefetch=1,
        grid=(nb, n_steps),
        in_specs=([pl.BlockSpec(memory_space=pltpu.SMEM), tok_spec]
                  + [page_spec(i) for i in range(gp)] * 2
                  + [new_spec, new_spec, pl.BlockSpec((1, DIFF_V_DIM), lambda b, g, pt: (0, 0))]),
        out_specs=tok_spec,
        scratch_shapes=[pltpu.VMEM((heads, 2 * dec_seq, 1), F32), pltpu.VMEM((heads, 2 * dec_seq, 1), F32),
                        pltpu.VMEM((heads, 2 * dec_seq, DIFF_V_DIM), F32)],
    )
    return pl.pallas_call(
        functools.partial(_paged_attn_kernel, heads=heads, dec_seq=dec_seq, n_pages=gp),
        grid_spec=grid_spec,
        out_shape=jax.ShapeDtypeStruct((t, dw), F32),
        compiler_params=_params(2),
        name="paged_diff_attn",
    )(page_ids, scal, qd, *([cache_k2] * gp), *([cache_v2] * gp), pad(k_new), pad(v_new),
      wsub.reshape(1, DIFF_V_DIM))


def _out_proj_kernel(x_ref, oh_ref, od_ref, wo_ref, wn_ref, wr_hi_ref, wr_lo_ref, br_ref,
                     x2_ref, xn_ref, lg_ref, *, hw):
    mix = (jnp.dot(oh_ref[...].astype(BF16), wo_ref[:hw, :], preferred_element_type=F32)
           + jnp.dot(od_ref[...].astype(BF16), wo_ref[hw:, :], preferred_element_type=F32))
    x2 = x_ref[...] + mix
    x2_ref[...] = x2
    xn = _rms(x2, wn_ref[...])
    xn_hi = xn.astype(BF16)
    xn_ref[...] = xn_hi
    xn_lo = (xn - xn_hi.astype(F32)).astype(BF16)
    lg_ref[...] = (jnp.dot(xn_hi, wr_hi_ref[...], preferred_element_type=F32)
                   + jnp.dot(xn_lo, wr_hi_ref[...], preferred_element_type=F32)
                   + jnp.dot(xn_hi, wr_lo_ref[...], preferred_element_type=F32)
                   + br_ref[...])


def _out_proj(x2d, o_h, o_d, wo_bf, wn, w_router, b_router):
    t, d = x2d.shape
    hw = o_h.shape[1]
    dw = o_d.shape[1]
    n_e = w_router.shape[1]
    tm = min(ROW_TILE, t)
    wr = jnp.pad(w_router.astype(F32), ((0, 0), (0, LANES - n_e)))
    wr_hi = wr.astype(BF16)
    wr_lo = (wr - wr_hi.astype(F32)).astype(BF16)
    br = jnp.pad(b_router.astype(F32), (0, LANES - n_e)).reshape(1, LANES)
    row = lambda i: (i, 0)
    fixed = lambda i: (0, 0)
    return pl.pallas_call(
        functools.partial(_out_proj_kernel, hw=hw),
        grid=(t // tm,),
        in_specs=[pl.BlockSpec((tm, d), row), pl.BlockSpec((tm, hw), row), pl.BlockSpec((tm, dw), row),
                  pl.BlockSpec(wo_bf.shape, fixed), pl.BlockSpec((1, d), fixed),
                  pl.BlockSpec((d, LANES), fixed), pl.BlockSpec((d, LANES), fixed), pl.BlockSpec((1, LANES), fixed)],
        out_specs=[pl.BlockSpec((tm, d), row), pl.BlockSpec((tm, d), row), pl.BlockSpec((tm, LANES), row)],
        out_shape=[jax.ShapeDtypeStruct((t, d), F32), jax.ShapeDtypeStruct((t, d), BF16),
                   jax.ShapeDtypeStruct((t, LANES), F32)],
        compiler_params=_params(1),
        name="out_proj_router",
    )(x2d, o_h, o_d, wo_bf, wn.reshape(1, d), wr_hi, wr_lo, br)


def _moe_ffn_kernel(be_ref, nu_ref, x_ref, wg_ref, bg_ref, wu_ref, bu_ref, wd_ref, bd_ref, y_ref):
    del be_ref
    i = pl.program_id(0)

    @pl.when(i < nu_ref[0])
    def _():
        x = x_ref[...]
        g = jnp.dot(x, wg_ref[0], preferred_element_type=F32) + bg_ref[0]
        u = jnp.dot(x, wu_ref[0], preferred_element_type=F32) + bu_ref[0]
        g = jnp.minimum(g, SWIGLU_LIMIT)
        u = jnp.clip(u, -SWIGLU_LIMIT, SWIGLU_LIMIT)
        h = (u + 1.0) * (g * jax.nn.sigmoid(SWIGLU_ALPHA * g))
        y_ref[...] = jnp.dot(h.astype(BF16), wd_ref[0], preferred_element_type=F32) + bd_ref[0]

    @pl.when(i >= nu_ref[0])
    def _():
        y_ref[...] = jnp.zeros(y_ref.shape, F32)


def _moe_ffn(xg, block_e, n_used, wg, bg, wu, bu, wd, bd):
    n_rows, d = xg.shape
    n_e, _, d_ff = wg.shape
    tm = MOE_TILE
    w_spec = lambda k, n: pl.BlockSpec((1, k, n), lambda i, be, nu: (be[i], 0, 0))
    grid_spec = pltpu.PrefetchScalarGridSpec(
        num_scalar_prefetch=2,
        grid=(n_rows // tm,),
        in_specs=[pl.BlockSpec((tm, d), lambda i, be, nu: (i, 0)),
                  w_spec(d, d_ff), w_spec(1, d_ff), w_spec(d, d_ff), w_spec(1, d_ff), w_spec(d_ff, d), w_spec(1, d)],
        out_specs=pl.BlockSpec((tm, d), lambda i, be, nu: (i, 0)),
    )
    return pl.pallas_call(
        _moe_ffn_kernel,
        grid_spec=grid_spec,
        out_shape=jax.ShapeDtypeStruct((n_rows, d), F32),
        compiler_params=_params(1),
        name="moe_ffn",
    )(block_e, n_used, xg, wg, bg.reshape(n_e, 1, d_ff), wu, bu.reshape(n_e, 1, d_ff), wd, bd.reshape(n_e, 1, d))


def _route(logits, n_e, tm):
    t = logits.shape[0]
    top_val, top_idx = lax.top_k(logits, TOP_K)
    gate = jax.nn.softmax(top_val, axis=-1)
    flat_e = top_idx.reshape(-1).astype(jnp.int32)
    n_assign = t * TOP_K
    onehot = (flat_e[:, None] == jnp.arange(n_e, dtype=jnp.int32)[None, :]).astype(jnp.int32)
    csum = jnp.cumsum(onehot, axis=0)
    rank = jnp.take_along_axis(csum, flat_e[:, None], axis=1)[:, 0] - 1
    counts = csum[-1]
    padded = (counts + tm - 1) // tm * tm
    pad_end = jnp.cumsum(padded)
    start_pad = pad_end - padded
    dest = start_pad[flat_e] + rank
    n_rows = (n_assign + n_e * (tm - 1) + tm - 1) // tm * tm
    n_blocks = n_rows // tm
    row_tok = jnp.zeros((n_rows,), jnp.int32).at[dest].set(jnp.arange(n_assign, dtype=jnp.int32) // TOP_K)
    block_e = jnp.minimum(jnp.searchsorted(pad_end, jnp.arange(n_blocks, dtype=jnp.int32) * tm, side='right'),
                          n_e - 1).astype(jnp.int32)
    n_used = (pad_end[-1:] // tm).astype(jnp.int32)
    return gate, dest.reshape(t, TOP_K), row_tok, block_e, n_used


def _combine_kernel(x2_ref, yg_ref, gate_ref, wn_ref, o_ref, *, d):
    gate = gate_ref[...]
    y = x2_ref[...]
    for k in range(TOP_K):
        y = y + yg_ref[:, k * d:(k + 1) * d] * gate[:, k:k + 1]
    o_ref[...] = _rms(y, wn_ref[...])


def _combine(x2, yg, gate, wn):
    t, d = x2.shape
    tm = min(COMBINE_TILE, t)
    row = lambda i: (i, 0)
    return pl.pallas_call(
        functools.partial(_combine_kernel, d=d),
        grid=(t // tm,),
        in_specs=[pl.BlockSpec((tm, d), row), pl.BlockSpec((tm, TOP_K * d), row), pl.BlockSpec((tm, TOP_K), row),
                  pl.BlockSpec((1, d), lambda i: (0, 0))],
        out_specs=pl.BlockSpec((tm, d), row),
        out_shape=jax.ShapeDtypeStruct((t, d), F32),
        compiler_params=_params(1),
        name="moe_combine",
    )(x2, yg, gate, wn.reshape(1, d))


def _layer_group(x, pos, s0, attend, p):
    nb, seq, d = x.shape
    t = nb * seq
    x2d = x.reshape(t, d)
    hw, dw = p["hw"], p["dw"]
    act_dtype = BF16 if seq % 16 == 0 else F32
    tm = min(ROW_TILE, t)
    tabs = _rope_tables(pos)
    if seq % tm == 0:
        tab_tiles = seq // tm
    else:
        tabs = tuple(jnp.tile(a, (nb, 1)) for a in tabs)
        tab_tiles = t // tm
    qh, f, vi, g, qd, k_rows, kb, v_rows, vb = _in_proj(
        x2d, p["w_norm_mix"], p["w_in"], p["lb"], tabs, tab_tiles, act_dtype, hw, dw)
    o_h, s_fin = _hgrn(qh, f, vi, g, p["w_hgrn_norm"], s0, nb, seq, act_dtype)
    o_d = attend(qd, kb, vb, nb, seq)
    x2, xn, logits = _out_proj(x2d, o_h, o_d, p["w_out"], p["w_norm_ffn"], p["w_router"], p["b_router"])
    n_e = p["w_router"].shape[1]
    gate, dest, row_tok, block_e, n_used = _route(logits[:, :n_e], n_e, MOE_TILE)
    yb = _moe_ffn(xn[row_tok], block_e, n_used, p["w_gate"], p["b_gate"], p["w_up"], p["b_up"],
                  p["w_down"], p["b_down"])
    y = _combine(x2, yb[dest].reshape(t, TOP_K * d), gate, p["w_norm_final"])
    heads = dw // DIFF_V_DIM
    return (y.reshape(nb, seq, d), k_rows.reshape(nb, seq, heads, DIFF_V_DIM),
            v_rows.reshape(nb, seq, heads, DIFF_V_DIM), s_fin)


def kernel(x_prompt, x_sample, cache_k, cache_v, state_hgrn, page_table, w_norm_mix, w_in, hgrn_lb_logits, w_hgrn_norm, diff_lambda_q1, diff_lambda_k1, diff_lambda_q2, diff_lambda_k2, w_subln, w_out, w_norm_ffn, w_router, b_router, w_gate, b_gate, w_up, b_up, w_down, b_down, w_norm_final):
    depth = w_in.shape[0]
    assert depth == 1, "single-layer trunk only"
    l = 0
    seq = x_prompt.shape[1]
    dec_batch, dec_seq = x_sample.shape[0], x_sample.shape[1]
    n_pool, page = cache_k.shape[1], cache_k.shape[2]
    past_len = page_table.shape[1] * page
    hw = w_hgrn_norm.shape[1]
    dw = w_in.shape[2] - 4 * hw
    dw //= 3
    lower_bounds = jnp.cumsum(jax.nn.softmax(hgrn_lb_logits.astype(F32), axis=0), axis=0)
    lam_init = 0.8 - 0.6 * math.exp(-0.3 * l)
    lam = (jnp.exp(jnp.sum(diff_lambda_q1[l].astype(F32) * diff_lambda_k1[l].astype(F32)))
           - jnp.exp(jnp.sum(diff_lambda_q2[l].astype(F32) * diff_lambda_k2[l].astype(F32))) + lam_init)
    scal = jnp.stack([lam, jnp.asarray(1.0 - lam_init, F32)]).astype(F32)
    p = dict(hw=hw, dw=dw, lb=lower_bounds[l], w_norm_mix=w_norm_mix[l], w_in=w_in[l].astype(BF16),
             w_hgrn_norm=w_hgrn_norm[l], w_out=w_out[l].astype(BF16), w_norm_ffn=w_norm_ffn[l],
             w_router=w_router[l], b_router=b_router[l],
             w_gate=w_gate[l].astype(BF16), b_gate=b_gate[l], w_up=w_up[l].astype(BF16), b_up=b_up[l],
             w_down=w_down[l].astype(BF16), b_down=b_down[l], w_norm_final=w_norm_final)
    wsub = w_subln[l]

    def attend_prompt(qd, kb, vb, nb, sq):
        return _causal_attn(qd, kb, vb, scal, wsub, nb, sq)

    cache_k2 = cache_k.reshape(depth * n_pool, page, cache_k.shape[3] * cache_k.shape[4])
    cache_v2 = cache_v.reshape(depth * n_pool, page, cache_v.shape[3] * cache_v.shape[4])
    page_ids = page_table.astype(jnp.int32) + l * n_pool

    def attend_sample(qd, kb, vb, nb, sq):
        return _paged_attn(qd, kb, vb, cache_k2, cache_v2, page_ids, scal, wsub, nb, sq)

    pos_prompt = jnp.arange(seq, dtype=jnp.int32)
    pos_sample = past_len + jnp.arange(dec_seq, dtype=jnp.int32)
    y_p, k_p, v_p, s_p = _layer_group(x_prompt, pos_prompt, None, attend_prompt, p)
    y_s, k_s, v_s, s_s = _layer_group(x_sample, pos_sample, state_hgrn[l], attend_sample, p)
    return (y_p, y_s, k_p[None], v_p[None], s_p[None], k_s[None], v_s[None], s_s[None])
```

```python
import functools
import math

import jax
import jax.numpy as jnp
from jax import lax
from jax.experimental import pallas as pl
from jax.experimental.pallas import tpu as pltpu

F32 = jnp.float32
BF16 = jnp.bfloat16

LANES = 128
HGRN_HEAD_DIM = 128
HGRN_CHUNK = 64
DIFF_HEAD_DIM = 64
DIFF_V_DIM = 2 * DIFF_HEAD_DIM
ROT_DIM = DIFF_HEAD_DIM // 4
ROPE_THETA = 500000.0
TOP_K = 4
SWIGLU_LIMIT = 7.0
SWIGLU_ALPHA = 1.702
NORM_EPS = 1e-6
VMEM_LIMIT = 56 * 1024 * 1024

ROW_TILE = 512
ATTN_TILE = 512
HGRN_TILE = 512
MOE_TILE = 256
PAGES_PER_STEP = 8
COMBINE_TILE = 256

_NT = (((1,), (1,)), ((), ()))


def _params(n_axes):
    return pltpu.CompilerParams(dimension_semantics=("arbitrary",) * n_axes, vmem_limit_bytes=VMEM_LIMIT)


def _rms(x, w):
    return x * lax.rsqrt(jnp.mean(x * x, axis=-1, keepdims=True) + NORM_EPS) * w


def _in_proj_kernel(x_ref, wn_ref, w_ref, lb_ref, rc_ref, rp_ref, rm_ref,
                    qh_ref, f_ref, vi_ref, g_ref, qd_ref, kr_ref, kb_ref, vr_ref, vb_ref, *, hw, dw):
    h = _rms(x_ref[...], wn_ref[...]).astype(BF16)

    def proj(c0, width):
        return jnp.dot(h, w_ref[:, c0:c0 + width], preferred_element_type=F32)

    hq = proj(0, hw)
    qh_ref[...] = hq * jax.nn.sigmoid(hq)
    lb = lb_ref[...]
    f_ref[...] = lb + (1.0 - lb) * jax.nn.sigmoid(proj(hw, hw))
    vi_ref[...] = proj(2 * hw, hw)
    hg = proj(3 * hw, hw)
    g_ref[...] = hg * jax.nn.sigmoid(hg)

    rc, rp, rm = rc_ref[...], rp_ref[...], rm_ref[...]
    half = ROT_DIM // 2

    def rope(t):
        return t * rc + pltpu.roll(t, half, 1) * rp + pltpu.roll(t, LANES - half, 1) * rm

    dq = proj(4 * hw, dw)
    dk = proj(4 * hw + dw, dw)
    dv = proj(4 * hw + 2 * dw, dw)
    scale = DIFF_HEAD_DIM ** -0.5
    heads = dw // LANES
    tm = dq.shape[0]
    for c in range(heads):
        sl = slice(c * LANES, (c + 1) * LANES)
        qd_ref[:, sl] = (rope(dq[:, sl]) * scale).astype(qd_ref.dtype)
        kr = rope(dk[:, sl])
        kr_ref[pl.ds(c, tm, stride=heads), :] = kr
        vr_ref[pl.ds(c, tm, stride=heads), :] = dv[:, sl]
        kb_ref[:, sl] = kr.astype(kb_ref.dtype)
    vb_ref[...] = dv.astype(vb_ref.dtype)


def _in_proj(x2d, wn, w_bf, lb, tabs, tab_tiles, act_dtype, hw, dw):
    t, d = x2d.shape
    tm = min(ROW_TILE, t)
    heads = dw // LANES
    row = lambda i: (i, 0)
    fixed = lambda i: (0, 0)
    tab = lambda i: (i % tab_tiles, 0)
    act = jax.ShapeDtypeStruct((t, dw), act_dtype)
    cache_rows = jax.ShapeDtypeStruct((t * heads, LANES), F32)
    out_shapes = [jax.ShapeDtypeStruct((t, hw), F32)] * 4 + [act, cache_rows, act, cache_rows, act]
    cache_spec = pl.BlockSpec((tm * heads, LANES), row)
    return pl.pallas_call(
        functools.partial(_in_proj_kernel, hw=hw, dw=dw),
        grid=(t // tm,),
        in_specs=[pl.BlockSpec((tm, d), row),
                  pl.BlockSpec((1, d), fixed),
                  pl.BlockSpec(w_bf.shape, fixed, pipeline_mode=pl.Buffered(1)),
                  pl.BlockSpec((1, hw), fixed),
                  pl.BlockSpec((tm, LANES), tab), pl.BlockSpec((tm, LANES), tab), pl.BlockSpec((tm, LANES), tab)],
        out_specs=([pl.BlockSpec((tm, hw), row)] * 4
                   + [pl.BlockSpec((tm, dw), row), cache_spec, pl.BlockSpec((tm, dw), row), cache_spec,
                      pl.BlockSpec((tm, dw), row)]),
        out_shape=out_shapes,
        compiler_params=_params(1),
        name="in_proj",
    )(x2d, wn.reshape(1, d), w_bf, lb.reshape(1, hw), *tabs)


def _rope_tables(pos):
    half = ROT_DIM // 2
    inv = ROPE_THETA ** (-jnp.arange(0, ROT_DIM, 2, dtype=F32) / ROT_DIM)
    ang = pos.astype(F32)[:, None] * inv[None, :]
    cos, sin = jnp.cos(ang), jnp.sin(ang)
    n = pos.shape[0]
    rest = DIFF_HEAD_DIM - ROT_DIM
    zeros_h = jnp.zeros((n, half), F32)
    rc = jnp.concatenate([cos, cos, jnp.ones((n, rest), F32)], axis=1)
    rp = jnp.concatenate([zeros_h, sin, jnp.zeros((n, rest), F32)], axis=1)
    rm = jnp.concatenate([-sin, zeros_h, jnp.zeros((n, rest), F32)], axis=1)
    reps = LANES // DIFF_HEAD_DIM
    return tuple(jnp.tile(a, (1, reps)) for a in (rc, rp, rm))


def _hgrn_kernel(*refs, has_state):
    if has_state:
        q_ref, f_ref, v_ref, g_ref, wn_ref, s0_ref, o_ref, sfin_ref, st_ref = refs
    else:
        q_ref, f_ref, v_ref, g_ref, wn_ref, o_ref, sfin_ref, st_ref = refs
    j = pl.program_id(2)
    c_len = HGRN_CHUNK
    kdim = HGRN_HEAD_DIM

    @pl.when(j == 0)
    def _():
        st_ref[...] = s0_ref[0, 0].T if has_state else jnp.zeros(st_ref.shape, F32)

    q, f, v = q_ref[...], f_ref[...], v_ref[...]
    rows = q.shape[0]
    if rows < c_len:
        pad = c_len - rows
        q = jnp.concatenate([q, jnp.zeros((pad, kdim), F32)], axis=0)
        v = jnp.concatenate([v, jnp.zeros((pad, kdim), F32)], axis=0)
        f = jnp.concatenate([f, jnp.ones((pad, kdim), F32)], axis=0)
    tl = q.shape[0]
    n_c = tl // c_len

    k = 1.0 - f
    b = jnp.log(f)
    r = lax.broadcasted_iota(jnp.int32, (tl, kdim), 0) % c_len
    s = 1
    while s < c_len:
        b = b + jnp.where(r >= s, pltpu.roll(b, s, 0), 0.0)
        s *= 2
    b3 = b.reshape(n_c, c_len, kdim)
    q3, k3, v3 = (a.reshape(n_c, c_len, kdim) for a in (q, k, v))
    bl = b3[:, c_len - 1:c_len, :]
    qe = (q3 * jnp.exp(b3)).astype(BF16)
    kd = (k3 * jnp.exp(-b3)).astype(BF16)
    kl = (k3 * jnp.exp(bl - b3)).astype(BF16)
    vb = v3.astype(BF16)
    dec = jnp.exp(bl)

    a = jnp.einsum('nck,ndk->ncd', qe, kd, preferred_element_type=F32)
    causal = (lax.broadcasted_iota(jnp.int32, (c_len, c_len), 1)
              <= lax.broadcasted_iota(jnp.int32, (c_len, c_len), 0))
    a = jnp.where(causal[None], a, 0.0).astype(BF16)
    o_intra = jnp.einsum('ncd,ndv->ncv', a, vb, preferred_element_type=F32)
    ds_t = jnp.einsum('ncv,nck->nvk', vb, kl, preferred_element_type=F32)

    st = st_ref[...]
    outs = []
    for c in range(n_c):
        o_inter = lax.dot_general(qe[c], st.astype(BF16), _NT, preferred_element_type=F32)
        outs.append(o_intra[c] + o_inter)
        st = st * dec[c] + ds_t[c]
    st_ref[...] = st
    o = outs[0] if n_c == 1 else jnp.concatenate(outs, axis=0)
    o = o[:rows]
    o_ref[...] = (_rms(o, wn_ref[...]) * g_ref[...]).astype(o_ref.dtype)

    @pl.when(j == pl.num_programs(2) - 1)
    def _():
        sfin_ref[0, 0] = st.T


def _hgrn(qh, f, vi, g, wn, s0, nb, seq, out_dtype):
    t, hw = qh.shape
    heads = hw // HGRN_HEAD_DIM
    tl = min(HGRN_TILE, seq)
    nl = seq // tl
    blk = pl.BlockSpec((tl, HGRN_HEAD_DIM), lambda b, h, j: (b * nl + j, h))
    state_blk = pl.BlockSpec((1, 1, HGRN_HEAD_DIM, HGRN_HEAD_DIM), lambda b, h, j: (b, h, 0, 0))
    in_specs = [blk, blk, blk, blk, pl.BlockSpec((1, HGRN_HEAD_DIM), lambda b, h, j: (0, h))]
    args = [qh, f, vi, g, wn.reshape(1, hw)]
    if s0 is not None:
        in_specs.append(state_blk)
        args.append(s0)
    return pl.pallas_call(
        functools.partial(_hgrn_kernel, has_state=s0 is not None),
        grid=(nb, heads, nl),
        in_specs=in_specs,
        out_specs=[blk, state_blk],
        out_shape=[jax.ShapeDtypeStruct((t, hw), out_dtype),
                   jax.ShapeDtypeStruct((nb, heads, HGRN_HEAD_DIM, HGRN_HEAD_DIM), F32)],
        scratch_shapes=[pltpu.VMEM((HGRN_HEAD_DIM, HGRN_HEAD_DIM), F32)],
        compiler_params=_params(3),
        name="hgrn",
    )(*args)


def _split_components(q):
    lane = lax.broadcasted_iota(jnp.int32, q.shape, 1)
    zero = jnp.zeros_like(q)
    return jnp.where(lane < DIFF_HEAD_DIM, q, zero), jnp.where(lane >= DIFF_HEAD_DIM, q, zero)


def _online_update(s, vb, m, l, acc):
    m_new = jnp.maximum(m, jnp.max(s, axis=-1, keepdims=True))
    alpha = jnp.exp(m - m_new)
    p = jnp.exp(s - m_new)
    l = alpha * l + jnp.sum(p, axis=-1, keepdims=True)
    acc = alpha * acc + jnp.dot(p.astype(BF16), vb, preferred_element_type=F32)
    return m_new, l, acc


def _diff_finish(o0, o1, sc_ref, wsub_ref):
    o = o0 - sc_ref[0] * o1
    return _rms(o, wsub_ref[...]) * sc_ref[1]


def _causal_attn_kernel(sc_ref, q_ref, k_ref, v_ref, wsub_ref, o_ref, *, tile):
    i = pl.program_id(2)
    q_lo, q_hi = _split_components(q_ref[...])

    def scores(kb):
        return (lax.dot_general(q_lo, kb, _NT, preferred_element_type=F32),
                lax.dot_general(q_hi, kb, _NT, preferred_element_type=F32))

    def body(j, carry):
        m0, l0, a0, m1, l1, a1 = carry
        start = pl.multiple_of(j * tile, tile)
        kb = k_ref[pl.ds(start, tile), :]
        vb = v_ref[pl.ds(start, tile), :]
        s0, s1 = scores(kb)
        return _online_update(s0, vb, m0, l0, a0) + _online_update(s1, vb, m1, l1, a1)

    neg = jnp.full((tile, 1), -jnp.inf, F32)
    zero1 = jnp.zeros((tile, 1), F32)
    zero_acc = jnp.zeros((tile, DIFF_V_DIM), F32)
    carry = lax.fori_loop(0, i, body, (neg, zero1, zero_acc, neg, zero1, zero_acc))

    start = pl.multiple_of(i * tile, tile)
    kb = k_ref[pl.ds(start, tile), :]
    vb = v_ref[pl.ds(start, tile), :]
    s0, s1 = scores(kb)
    keep = (lax.broadcasted_iota(jnp.int32, (tile, tile), 1) <= lax.broadcasted_iota(jnp.int32, (tile, tile), 0))
    s0 = jnp.where(keep, s0, -jnp.inf)
    s1 = jnp.where(keep, s1, -jnp.inf)
    m0, l0, a0, m1, l1, a1 = carry
    _, l0, a0 = _online_update(s0, vb, m0, l0, a0)
    _, l1, a1 = _online_update(s1, vb, m1, l1, a1)
    o_ref[...] = _diff_finish(a0 / l0, a1 / l1, sc_ref, wsub_ref).astype(o_ref.dtype)


def _causal_attn(qd, kb, vb, scal, wsub, nb, seq):
    t, dw = qd.shape
    heads = dw // DIFF_V_DIM
    tile = min(ATTN_TILE, seq)
    nq = seq // tile
    return pl.pallas_call(
        functools.partial(_causal_attn_kernel, tile=tile),
        grid=(nb, heads, nq),
        in_specs=[pl.BlockSpec(memory_space=pltpu.SMEM),
                  pl.BlockSpec((tile, DIFF_V_DIM), lambda b, h, i: (b * nq + i, h)),
                  pl.BlockSpec((seq, DIFF_V_DIM), lambda b, h, i: (b, h)),
                  pl.BlockSpec((seq, DIFF_V_DIM), lambda b, h, i: (b, h)),
                  pl.BlockSpec((1, DIFF_V_DIM), lambda b, h, i: (0, 0))],
        out_specs=pl.BlockSpec((tile, DIFF_V_DIM), lambda b, h, i: (b * nq + i, h)),
        out_shape=jax.ShapeDtypeStruct((t, dw), BF16),
        compiler_params=_params(3),
        name="causal_diff_attn",
    )(scal, qd, kb, vb, wsub.reshape(1, DIFF_V_DIM))


def _paged_attn_kernel(*refs, heads, dec_seq, n_pages):
    pt_ref, sc_ref, q_ref = refs[0], refs[1], refs[2]
    k_pages = refs[3:3 + n_pages]
    v_pages = refs[3 + n_pages:3 + 2 * n_pages]
    kn_ref, vn_ref, wsub_ref, o_ref, m_ref, l_ref, acc_ref = refs[3 + 2 * n_pages:]
    del pt_ref
    g = pl.program_id(1)
    per_head = 2 * dec_seq
    rows = heads * per_head

    @pl.when(g == 0)
    def _():
        m_ref[...] = jnp.full(m_ref.shape, -jnp.inf, F32)
        l_ref[...] = jnp.zeros(l_ref.shape, F32)
        acc_ref[...] = jnp.zeros(acc_ref.shape, F32)

    q = q_ref[...].astype(BF16)
    parts = []
    for h in range(heads):
        parts.extend(_split_components(q[:, h * DIFF_V_DIM:(h + 1) * DIFF_V_DIM]))
    q_all = jnp.concatenate(parts, axis=0)

    def accumulate(kb, vb, keep):
        s = lax.dot_general(q_all, kb, _NT, preferred_element_type=F32)
        m, l, acc = _online_update(jnp.where(keep, s, -jnp.inf), vb, m_ref[...], l_ref[...], acc_ref[...])
        m_ref[...], l_ref[...], acc_ref[...] = m, l, acc

    n_keys = n_pages * k_pages[0].shape[0]
    col = lax.broadcasted_iota(jnp.int32, (rows, n_keys), 1)
    row = lax.broadcasted_iota(jnp.int32, (rows, n_keys), 0)
    accumulate(jnp.concatenate([kp[...].astype(BF16) for kp in k_pages], axis=0),
               jnp.concatenate([vp[...].astype(BF16) for vp in v_pages], axis=0),
               col % heads == row // per_head)

    @pl.when(g == pl.num_programs(1) - 1)
    def _():
        n_new = kn_ref.shape[1]
        col = lax.broadcasted_iota(jnp.int32, (rows, n_new), 1)
        row = lax.broadcasted_iota(jnp.int32, (rows, n_new), 0)
        keep = (col % heads == row // per_head) & (col // heads <= row % dec_seq)
        accumulate(kn_ref[0].astype(BF16), vn_ref[0].astype(BF16), keep)
        o = acc_ref[...] / l_ref[...]
        for h in range(heads):
            o0 = o[h * per_head:h * per_head + dec_seq]
            o1 = o[h * per_head + dec_seq:(h + 1) * per_head]
            o_ref[:, h * DIFF_V_DIM:(h + 1) * DIFF_V_DIM] = _diff_finish(o0, o1, sc_ref, wsub_ref).astype(o_ref.dtype)


def _paged_attn(qd, k_new, v_new, cache_k2, cache_v2, page_ids, page, scal, wsub, nb, dec_seq):
    t, dw = qd.shape
    heads = dw // DIFF_V_DIM
    page_rows = page * heads
    n_pages = page_ids.shape[1]
    gp = min(PAGES_PER_STEP, n_pages)
    n_steps = n_pages // gp
    new_rows = -(-dec_seq * heads // LANES) * LANES
    pad = lambda a: jnp.pad(a.reshape(nb, dec_seq * heads, DIFF_V_DIM), ((0, 0), (0, new_rows - dec_seq * heads), (0, 0)))
    page_spec = lambda i: pl.BlockSpec((page_rows, DIFF_V_DIM), lambda b, g, pt: (pt[b, g * gp + i], 0))
    tok_spec = pl.BlockSpec((dec_seq, dw), lambda b, g, pt: (b, 0))
    new_spec = pl.BlockSpec((1, new_rows, DIFF_V_DIM), lambda b, g, pt: (b, 0, 0))
    rows = heads * 2 * dec_seq
    grid_spec = pltpu.PrefetchScalarGridSpec(
        num_scalar_prefetch=1,
        grid=(nb, n_steps),
        in_specs=([pl.BlockSpec(memory_space=pltpu.SMEM), tok_spec]
                  + [page_spec(i) for i in range(gp)] * 2
                  + [new_spec, new_spec, pl.BlockSpec((1, DIFF_V_DIM), lambda b, g, pt: (0, 0))]),
        out_specs=tok_spec,
        scratch_shapes=[pltpu.VMEM((rows, 1), F32), pltpu.VMEM((rows, 1), F32), pltpu.VMEM((rows, DIFF_V_DIM), F32)],
    )
    return pl.pallas_call(
        functools.partial(_paged_attn_kernel, heads=heads, dec_seq=dec_seq, n_pages=gp),
        grid_spec=grid_spec,
        out_shape=jax.ShapeDtypeStruct((t, dw), F32),
        compiler_params=_params(2),
        name="paged_diff_attn",
    )(page_ids, scal, qd, *([cache_k2] * gp), *([cache_v2] * gp), pad(k_new), pad(v_new),
      wsub.reshape(1, DIFF_V_DIM))


def _out_proj_kernel(x_ref, oh_ref, od_ref, wo_ref, wn_ref, wr_hi_ref, wr_lo_ref, br_ref,
                     x2_ref, xn_ref, lg_ref, *, hw):
    mix = (jnp.dot(oh_ref[...].astype(BF16), wo_ref[:hw, :], preferred_element_type=F32)
           + jnp.dot(od_ref[...].astype(BF16), wo_ref[hw:, :], preferred_element_type=F32))
    x2 = x_ref[...] + mix
    x2_ref[...] = x2
    xn = _rms(x2, wn_ref[...])
    xn_hi = xn.astype(BF16)
    xn_ref[...] = xn_hi
    xn_lo = (xn - xn_hi.astype(F32)).astype(BF16)
    lg_ref[...] = (jnp.dot(xn_hi, wr_hi_ref[...], preferred_element_type=F32)
                   + jnp.dot(xn_lo, wr_hi_ref[...], preferred_element_type=F32)
                   + jnp.dot(xn_hi, wr_lo_ref[...], preferred_element_type=F32)
                   + br_ref[...])


def _out_proj(x2d, o_h, o_d, wo_bf, wn, w_router, b_router):
    t, d = x2d.shape
    hw = o_h.shape[1]
    dw = o_d.shape[1]
    n_e = w_router.shape[1]
    tm = min(ROW_TILE, t)
    wr = jnp.pad(w_router.astype(F32), ((0, 0), (0, LANES - n_e)))
    wr_hi = wr.astype(BF16)
    wr_lo = (wr - wr_hi.astype(F32)).astype(BF16)
    br = jnp.pad(b_router.astype(F32), (0, LANES - n_e), constant_values=-jnp.inf).reshape(1, LANES)
    row = lambda i: (i, 0)
    fixed = lambda i: (0, 0)
    return pl.pallas_call(
        functools.partial(_out_proj_kernel, hw=hw),
        grid=(t // tm,),
        in_specs=[pl.BlockSpec((tm, d), row), pl.BlockSpec((tm, hw), row), pl.BlockSpec((tm, dw), row),
                  pl.BlockSpec(wo_bf.shape, fixed), pl.BlockSpec((1, d), fixed),
                  pl.BlockSpec((d, LANES), fixed), pl.BlockSpec((d, LANES), fixed), pl.BlockSpec((1, LANES), fixed)],
        out_specs=[pl.BlockSpec((tm, d), row), pl.BlockSpec((tm, d), row), pl.BlockSpec((tm, LANES), row)],
        out_shape=[jax.ShapeDtypeStruct((t, d), F32), jax.ShapeDtypeStruct((t, d), BF16),
                   jax.ShapeDtypeStruct((t, LANES), F32)],
        compiler_params=_params(1),
        name="out_proj_router",
    )(x2d, o_h, o_d, wo_bf, wn.reshape(1, d), wr_hi, wr_lo, br)


def _router_kernel(lg_ref, gate_ref, idx_ref, rank_ref, cnt_ref, base_ref):
    i = pl.program_id(0)

    @pl.when(i == 0)
    def _():
        base_ref[...] = jnp.zeros(base_ref.shape, F32)

    cur = lg_ref[...]
    tm = cur.shape[0]
    lane = lax.broadcasted_iota(jnp.int32, cur.shape, 1)
    vals, idxs, hits = [], [], []
    for _ in range(TOP_K):
        mx = jnp.max(cur, axis=-1, keepdims=True)
        ix = jnp.min(jnp.where(cur == mx, lane, LANES), axis=-1, keepdims=True)
        hit = lane == ix
        cur = jnp.where(hit, -jnp.inf, cur)
        vals.append(mx)
        idxs.append(ix)
        hits.append(hit)
    es = [jnp.exp(v - vals[0]) for v in vals]
    denom = functools.reduce(lambda a, b: a + b, es)

    chosen = functools.reduce(lambda a, b: a | b, hits)
    chosen_f = jnp.where(chosen, 1.0, 0.0)
    below = (lax.broadcasted_iota(jnp.int32, (tm, tm), 1) < lax.broadcasted_iota(jnp.int32, (tm, tm), 0))
    earlier = base_ref[...] + jnp.dot(jnp.where(below, 1.0, 0.0).astype(BF16), chosen_f.astype(BF16),
                                      preferred_element_type=F32)
    lane_k = lax.broadcasted_iota(jnp.int32, (tm, TOP_K), 1)
    gate = jnp.zeros((tm, TOP_K), F32)
    idx = jnp.zeros((tm, TOP_K), jnp.int32)
    rank = jnp.zeros((tm, TOP_K), jnp.int32)
    for k in range(TOP_K):
        rk = jnp.sum(jnp.where(hits[k], earlier, 0.0), axis=-1, keepdims=True).astype(jnp.int32)
        gate = jnp.where(lane_k == k, es[k] / denom, gate)
        idx = jnp.where(lane_k == k, idxs[k], idx)
        rank = jnp.where(lane_k == k, rk, rank)
    gate_ref[...] = gate
    idx_ref[...] = idx
    rank_ref[...] = rank
    total = base_ref[...] + jnp.sum(chosen_f, axis=0, keepdims=True)
    base_ref[...] = total
    cnt_ref[...] = total.astype(jnp.int32)


def _router(logits):
    t = logits.shape[0]
    tm = min(ROW_TILE, t)
    row = lambda i: (i, 0)
    return pl.pallas_call(
        _router_kernel,
        grid=(t // tm,),
        in_specs=[pl.BlockSpec((tm, LANES), row)],
        out_specs=[pl.BlockSpec((tm, TOP_K), row)] * 3 + [pl.BlockSpec((1, LANES), lambda i: (0, 0))],
        out_shape=[jax.ShapeDtypeStruct((t, TOP_K), F32), jax.ShapeDtypeStruct((t, TOP_K), jnp.int32),
                   jax.ShapeDtypeStruct((t, TOP_K), jnp.int32), jax.ShapeDtypeStruct((1, LANES), jnp.int32)],
        scratch_shapes=[pltpu.VMEM((1, LANES), F32)],
        compiler_params=_params(1),
        name="router_topk",
    )(logits)


def _moe_ffn_kernel(be_ref, nu_ref, x_ref, wg_ref, bg_ref, wu_ref, bu_ref, wd_ref, bd_ref, y_ref,
                    wg_bf, wu_bf, wd_bf):
    i = pl.program_id(0)
    used = i < nu_ref[0]

    @pl.when(used & ((i == 0) | (be_ref[i] != be_ref[jnp.maximum(i - 1, 0)])))
    def _():
        wg_bf[...] = wg_ref[0].astype(BF16)
        wu_bf[...] = wu_ref[0].astype(BF16)
        wd_bf[...] = wd_ref[0].astype(BF16)

    @pl.when(used)
    def _():
        x = x_ref[...]
        g = jnp.dot(x, wg_bf[...], preferred_element_type=F32) + bg_ref[0]
        u = jnp.dot(x, wu_bf[...], preferred_element_type=F32) + bu_ref[0]
        g = jnp.minimum(g, SWIGLU_LIMIT)
        u = jnp.clip(u, -SWIGLU_LIMIT, SWIGLU_LIMIT)
        h = (u + 1.0) * (g * jax.nn.sigmoid(SWIGLU_ALPHA * g))
        y = jnp.dot(h.astype(BF16), wd_bf[...], preferred_element_type=F32) + bd_ref[0]
        y_ref[...] = y.astype(y_ref.dtype)

    @pl.when(jnp.logical_not(used))
    def _():
        y_ref[...] = jnp.zeros(y_ref.shape, y_ref.dtype)


def _moe_ffn(xg, block_e, n_used, wg, bg, wu, bu, wd, bd):
    n_rows, d = xg.shape
    n_e, _, d_ff = wg.shape
    tm = MOE_TILE
    w_spec = lambda k, n: pl.BlockSpec((1, k, n), lambda i, be, nu: (be[i], 0, 0))
    grid_spec = pltpu.PrefetchScalarGridSpec(
        num_scalar_prefetch=2,
        grid=(n_rows // tm,),
        in_specs=[pl.BlockSpec((tm, d), lambda i, be, nu: (i, 0)),
                  w_spec(d, d_ff), w_spec(1, d_ff), w_spec(d, d_ff), w_spec(1, d_ff), w_spec(d_ff, d), w_spec(1, d)],
        out_specs=pl.BlockSpec((tm, d), lambda i, be, nu: (i, 0)),
        scratch_shapes=[pltpu.VMEM((d, d_ff), BF16), pltpu.VMEM((d, d_ff), BF16), pltpu.VMEM((d_ff, d), BF16)],
    )
    return pl.pallas_call(
        _moe_ffn_kernel,
        grid_spec=grid_spec,
        out_shape=jax.ShapeDtypeStruct((n_rows, d), BF16),
        compiler_params=_params(1),
        name="moe_ffn",
    )(block_e, n_used, xg, wg, bg.reshape(n_e, 1, d_ff), wu, bu.reshape(n_e, 1, d_ff), wd, bd.reshape(n_e, 1, d))


def _plan_rows(idx, rank, counts, tm):
    t = idx.shape[0]
    n_e = counts.shape[0]
    n_assign = t * TOP_K
    padded = (counts + tm - 1) // tm * tm
    pad_end = jnp.cumsum(padded)
    start_pad = pad_end - padded
    start_sorted = jnp.cumsum(counts) - counts
    dest = start_pad[idx] + rank
    n_rows = (n_assign + n_e * (tm - 1) + tm - 1) // tm * tm
    n_blocks = n_rows // tm
    block_e = jnp.minimum(jnp.searchsorted(pad_end, jnp.arange(n_blocks, dtype=jnp.int32) * tm, side='right'),
                          n_e - 1).astype(jnp.int32)
    tok = jnp.arange(n_assign, dtype=jnp.int32) // TOP_K
    _, sorted_tok = lax.sort((dest.reshape(-1), tok), num_keys=1)
    r = jnp.arange(n_rows, dtype=jnp.int32)
    row_e = jnp.repeat(block_e, tm)
    within = r - start_pad[row_e]
    src = jnp.clip(start_sorted[row_e] + within, 0, n_assign - 1)
    row_tok = jnp.where(within < counts[row_e], sorted_tok[src], 0)
    n_used = (pad_end[-1:] // tm).astype(jnp.int32)
    return dest, row_tok, block_e, n_used


def _combine_kernel(x2_ref, yg_ref, gate_ref, wn_ref, o_ref):
    gate = gate_ref[...]
    y = x2_ref[...]
    for k in range(TOP_K):
        y = y + yg_ref[k].astype(F32) * gate[:, k:k + 1]
    o_ref[...] = _rms(y, wn_ref[...])


def _combine(x2, yg, gate, wn):
    t, d = x2.shape
    tm = min(COMBINE_TILE, t)
    row = lambda i: (i, 0)
    return pl.pallas_call(
        _combine_kernel,
        grid=(t // tm,),
        in_specs=[pl.BlockSpec((tm, d), row), pl.BlockSpec((TOP_K, tm, d), lambda i: (0, i, 0)),
                  pl.BlockSpec((tm, TOP_K), row), pl.BlockSpec((1, d), lambda i: (0, 0))],
        out_specs=pl.BlockSpec((tm, d), row),
        out_shape=jax.ShapeDtypeStruct((t, d), F32),
        compiler_params=_params(1),
        name="moe_combine",
    )(x2, yg, gate, wn.reshape(1, d))


def _layer_group(x, pos, s0, attend, p):
    nb, seq, d = x.shape
    t = nb * seq
    x2d = x.reshape(t, d)
    hw, dw = p["hw"], p["dw"]
    act_dtype = BF16 if seq % 16 == 0 else F32
    tm = min(ROW_TILE, t)
    tabs = _rope_tables(pos)
    if seq % tm == 0:
        tab_tiles = seq // tm
    else:
        tabs = tuple(jnp.tile(a, (nb, 1)) for a in tabs)
        tab_tiles = t // tm
    qh, f, vi, g, qd, k_rows, kb, v_rows, vb = _in_proj(
        x2d, p["w_norm_mix"], p["w_in"], p["lb"], tabs, tab_tiles, act_dtype, hw, dw)
    o_h, s_fin = _hgrn(qh, f, vi, g, p["w_hgrn_norm"], s0, nb, seq, act_dtype)
    o_d = attend(qd, kb, vb, k_rows, v_rows, nb, seq)
    x2, xn, logits = _out_proj(x2d, o_h, o_d, p["w_out"], p["w_norm_ffn"], p["w_router"], p["b_router"])
    n_e = p["w_router"].shape[1]
    gate, idx, rank, counts = _router(logits)
    dest, row_tok, block_e, n_used = _plan_rows(idx, rank, counts[0, :n_e], MOE_TILE)
    yb = _moe_ffn(xn[row_tok], block_e, n_used, p["w_gate"], p["b_gate"], p["w_up"], p["b_up"],
                  p["w_down"], p["b_down"])
    y = _combine(x2, yb[dest.T], gate, p["w_norm_final"])
    heads = dw // DIFF_V_DIM
    return (y.reshape(nb, seq, d), k_rows.reshape(nb, seq, heads, DIFF_V_DIM),
            v_rows.reshape(nb, seq, heads, DIFF_V_DIM), s_fin)


def kernel(x_prompt, x_sample, cache_k, cache_v, state_hgrn, page_table, w_norm_mix, w_in, hgrn_lb_logits, w_hgrn_norm, diff_lambda_q1, diff_lambda_k1, diff_lambda_q2, diff_lambda_k2, w_subln, w_out, w_norm_ffn, w_router, b_router, w_gate, b_gate, w_up, b_up, w_down, b_down, w_norm_final):
    depth = w_in.shape[0]
    assert depth == 1, "single-layer trunk only"
    l = 0
    seq = x_prompt.shape[1]
    dec_seq = x_sample.shape[1]
    n_pool, page = cache_k.shape[1], cache_k.shape[2]
    past_len = page_table.shape[1] * page
    hw = w_hgrn_norm.shape[1]
    dw = (w_in.shape[2] - 4 * hw) // 3
    lower_bounds = jnp.cumsum(jax.nn.softmax(hgrn_lb_logits.astype(F32), axis=0), axis=0)
    lam_init = 0.8 - 0.6 * math.exp(-0.3 * l)
    lam = (jnp.exp(jnp.sum(diff_lambda_q1[l].astype(F32) * diff_lambda_k1[l].astype(F32)))
           - jnp.exp(jnp.sum(diff_lambda_q2[l].astype(F32) * diff_lambda_k2[l].astype(F32))) + lam_init)
    scal = jnp.stack([lam, jnp.asarray(1.0 - lam_init, F32)]).astype(F32)
    p = dict(hw=hw, dw=dw, lb=lower_bounds[l], w_norm_mix=w_norm_mix[l], w_in=w_in[l].astype(BF16),
             w_hgrn_norm=w_hgrn_norm[l], w_out=w_out[l].astype(BF16), w_norm_ffn=w_norm_ffn[l],
             w_router=w_router[l], b_router=b_router[l],
             w_gate=w_gate[l], b_gate=b_gate[l], w_up=w_up[l], b_up=b_up[l],
             w_down=w_down[l], b_down=b_down[l], w_norm_final=w_norm_final)
    wsub = w_subln[l]

    def attend_prompt(qd, kb, vb, k_rows, v_rows, nb, sq):
        return _causal_attn(qd, kb, vb, scal, wsub, nb, sq)

    cache_k2 = cache_k.reshape(-1, cache_k.shape[-1])
    cache_v2 = cache_v.reshape(-1, cache_v.shape[-1])
    page_ids = page_table.astype(jnp.int32) + l * n_pool

    def attend_sample(qd, kb, vb, k_rows, v_rows, nb, sq):
        return _paged_attn(qd, k_rows, v_rows, cache_k2, cache_v2, page_ids, page, scal, wsub, nb, sq)

    pos_prompt = jnp.arange(seq, dtype=jnp.int32)
    pos_sample = past_len + jnp.arange(dec_seq, dtype=jnp.int32)
    y_p, k_p, v_p, s_p = _layer_group(x_prompt, pos_prompt, None, attend_prompt, p)
    y_s, k_s, v_s, s_s = _layer_group(x_sample, pos_sample, state_hgrn[l], attend_sample, p)
    return (y_p, y_s, k_p[None], v_p[None], s_p[None], k_s[None], v_s[None], s_s[None])
```

```python
import functools
import math

import jax
import jax.numpy as jnp
from jax import lax
from jax.experimental import pallas as pl
from jax.experimental.pallas import tpu as pltpu

F32 = jnp.float32
BF16 = jnp.bfloat16

LANES = 128
HGRN_HEAD_DIM = 128
HGRN_CHUNK = 64
DIFF_HEAD_DIM = 64
DIFF_V_DIM = 2 * DIFF_HEAD_DIM
ROT_DIM = DIFF_HEAD_DIM // 4
ROPE_THETA = 500000.0
TOP_K = 4
SWIGLU_LIMIT = 7.0
SWIGLU_ALPHA = 1.702
NORM_EPS = 1e-6
VMEM_LIMIT = 56 * 1024 * 1024

ROW_TILE = 512
ATTN_TILE = 512
HGRN_TILE = 512
MOE_TILE = 256
PAGES_PER_STEP = 8
COMBINE_TILE = 256
PROMPT_CHUNKS = 2
ONES_ROWS = 16
QUERY_SCALE = DIFF_HEAD_DIM ** -0.5 * math.log2(math.e)

_NT = (((1,), (1,)), ((), ()))


def _params(n_axes):
    return pltpu.CompilerParams(dimension_semantics=("arbitrary",) * n_axes, vmem_limit_bytes=VMEM_LIMIT)


def _rms(x, w):
    return x * lax.rsqrt(jnp.mean(x * x, axis=-1, keepdims=True) + NORM_EPS) * w


def _in_proj_kernel(x_ref, wn_ref, w_ref, lb_ref, rc_ref, rp_ref, rm_ref,
                    qh_ref, f_ref, vi_ref, g_ref, qd_ref, kr_ref, kb_ref, vr_ref, vb_ref, *, hw, dw):
    h = _rms(x_ref[...], wn_ref[...]).astype(BF16)

    def proj(c0, width):
        return jnp.dot(h, w_ref[:, c0:c0 + width], preferred_element_type=F32)

    hq = proj(0, hw)
    qh_ref[...] = hq * jax.nn.sigmoid(hq)
    lb = lb_ref[...]
    f_ref[...] = lb + (1.0 - lb) * jax.nn.sigmoid(proj(hw, hw))
    vi_ref[...] = proj(2 * hw, hw)
    hg = proj(3 * hw, hw)
    g_ref[...] = hg * jax.nn.sigmoid(hg)

    rc, rp, rm = rc_ref[...], rp_ref[...], rm_ref[...]
    half = ROT_DIM // 2

    def rope(t):
        return t * rc + pltpu.roll(t, half, 1) * rp + pltpu.roll(t, LANES - half, 1) * rm

    dq = proj(4 * hw, dw)
    dk = proj(4 * hw + dw, dw)
    dv = proj(4 * hw + 2 * dw, dw)
    scale = QUERY_SCALE
    heads = dw // LANES
    tm = dq.shape[0]
    for c in range(heads):
        sl = slice(c * LANES, (c + 1) * LANES)
        qd_ref[:, sl] = (rope(dq[:, sl]) * scale).astype(qd_ref.dtype)
        kr = rope(dk[:, sl])
        kr_ref[pl.ds(c, tm, stride=heads), :] = kr
        vr_ref[pl.ds(c, tm, stride=heads), :] = dv[:, sl]
        kb_ref[:, sl] = kr.astype(kb_ref.dtype)
    vb_ref[...] = dv.astype(vb_ref.dtype)


def _in_proj(x2d, row0, t, wn, w_bf, lb, tabs, tab_tiles, act_dtype, hw, dw):
    d = x2d.shape[1]
    tm = min(ROW_TILE, t)
    heads = dw // LANES
    tile0 = row0 // tm
    row = lambda i: (i, 0)
    fixed = lambda i: (0, 0)
    tab = lambda i: (i % tab_tiles, 0)
    act = jax.ShapeDtypeStruct((t, dw), act_dtype)
    cache_rows = jax.ShapeDtypeStruct((t * heads, LANES), F32)
    out_shapes = [jax.ShapeDtypeStruct((t, hw), F32)] * 4 + [act, cache_rows, act, cache_rows, act]
    cache_spec = pl.BlockSpec((tm * heads, LANES), row)
    return pl.pallas_call(
        functools.partial(_in_proj_kernel, hw=hw, dw=dw),
        grid=(t // tm,),
        in_specs=[pl.BlockSpec((tm, d), lambda i: (tile0 + i, 0)),
                  pl.BlockSpec((1, d), fixed),
                  pl.BlockSpec(w_bf.shape, fixed, pipeline_mode=pl.Buffered(1)),
                  pl.BlockSpec((1, hw), fixed),
                  pl.BlockSpec((tm, LANES), tab), pl.BlockSpec((tm, LANES), tab), pl.BlockSpec((tm, LANES), tab)],
        out_specs=([pl.BlockSpec((tm, hw), row)] * 4
                   + [pl.BlockSpec((tm, dw), row), cache_spec, pl.BlockSpec((tm, dw), row), cache_spec,
                      pl.BlockSpec((tm, dw), row)]),
        out_shape=out_shapes,
        compiler_params=_params(1),
        name="in_proj",
    )(x2d, wn.reshape(1, d), w_bf, lb.reshape(1, hw), *tabs)


def _rope_tables(pos):
    half = ROT_DIM // 2
    inv = ROPE_THETA ** (-jnp.arange(0, ROT_DIM, 2, dtype=F32) / ROT_DIM)
    ang = pos.astype(F32)[:, None] * inv[None, :]
    cos, sin = jnp.cos(ang), jnp.sin(ang)
    n = pos.shape[0]
    rest = DIFF_HEAD_DIM - ROT_DIM
    zeros_h = jnp.zeros((n, half), F32)
    rc = jnp.concatenate([cos, cos, jnp.ones((n, rest), F32)], axis=1)
    rp = jnp.concatenate([zeros_h, sin, jnp.zeros((n, rest), F32)], axis=1)
    rm = jnp.concatenate([-sin, zeros_h, jnp.zeros((n, rest), F32)], axis=1)
    reps = LANES // DIFF_HEAD_DIM
    return tuple(jnp.tile(a, (1, reps)) for a in (rc, rp, rm))


def _hgrn_kernel(*refs, has_state):
    if has_state:
        q_ref, f_ref, v_ref, g_ref, wn_ref, s0_ref, o_ref, sfin_ref, st_ref = refs
    else:
        q_ref, f_ref, v_ref, g_ref, wn_ref, o_ref, sfin_ref, st_ref = refs
    j = pl.program_id(2)
    c_len = HGRN_CHUNK
    kdim = HGRN_HEAD_DIM

    @pl.when(j == 0)
    def _():
        st_ref[...] = s0_ref[0, 0].T if has_state else jnp.zeros(st_ref.shape, F32)

    q, f, v = q_ref[...], f_ref[...], v_ref[...]
    rows = q.shape[0]
    if rows < c_len:
        pad = c_len - rows
        q = jnp.concatenate([q, jnp.zeros((pad, kdim), F32)], axis=0)
        v = jnp.concatenate([v, jnp.zeros((pad, kdim), F32)], axis=0)
        f = jnp.concatenate([f, jnp.ones((pad, kdim), F32)], axis=0)
    tl = q.shape[0]
    n_c = tl // c_len

    k = 1.0 - f
    b = jnp.log(f)
    r = lax.broadcasted_iota(jnp.int32, (tl, kdim), 0) % c_len
    s = 1
    while s < c_len:
        b = b + jnp.where(r >= s, pltpu.roll(b, s, 0), 0.0)
        s *= 2
    b3 = b.reshape(n_c, c_len, kdim)
    q3, k3, v3 = (a.reshape(n_c, c_len, kdim) for a in (q, k, v))
    bl = b3[:, c_len - 1:c_len, :]
    qe = (q3 * jnp.exp(b3)).astype(BF16)
    kd = (k3 * jnp.exp(-b3)).astype(BF16)
    kl = (k3 * jnp.exp(bl - b3)).astype(BF16)
    vb = v3.astype(BF16)
    dec = jnp.exp(bl)

    a = jnp.einsum('nck,ndk->ncd', qe, kd, preferred_element_type=F32)
    causal = (lax.broadcasted_iota(jnp.int32, (c_len, c_len), 1)
              <= lax.broadcasted_iota(jnp.int32, (c_len, c_len), 0))
    a = jnp.where(causal[None], a, 0.0).astype(BF16)
    o_intra = jnp.einsum('ncd,ndv->ncv', a, vb, preferred_element_type=F32)
    ds_t = jnp.einsum('ncv,nck->nvk', vb, kl, preferred_element_type=F32)

    st = st_ref[...]
    outs = []
    for c in range(n_c):
        o_inter = lax.dot_general(qe[c], st.astype(BF16), _NT, preferred_element_type=F32)
        outs.append(o_intra[c] + o_inter)
        st = st * dec[c] + ds_t[c]
    st_ref[...] = st
    o = outs[0] if n_c == 1 else jnp.concatenate(outs, axis=0)
    o = o[:rows]
    o_ref[...] = (_rms(o, wn_ref[...]) * g_ref[...]).astype(o_ref.dtype)

    @pl.when(j == pl.num_programs(2) - 1)
    def _():
        sfin_ref[0, 0] = st.T


def _hgrn(qh, f, vi, g, wn, s0, nb, seq, out_dtype):
    t, hw = qh.shape
    heads = hw // HGRN_HEAD_DIM
    tl = min(HGRN_TILE, seq)
    nl = seq // tl
    blk = pl.BlockSpec((tl, HGRN_HEAD_DIM), lambda b, h, j: (b * nl + j, h))
    state_blk = pl.BlockSpec((1, 1, HGRN_HEAD_DIM, HGRN_HEAD_DIM), lambda b, h, j: (b, h, 0, 0))
    in_specs = [blk, blk, blk, blk, pl.BlockSpec((1, HGRN_HEAD_DIM), lambda b, h, j: (0, h))]
    args = [qh, f, vi, g, wn.reshape(1, hw)]
    if s0 is not None:
        in_specs.append(state_blk)
        args.append(s0)
    return pl.pallas_call(
        functools.partial(_hgrn_kernel, has_state=s0 is not None),
        grid=(nb, heads, nl),
        in_specs=in_specs,
        out_specs=[blk, state_blk],
        out_shape=[jax.ShapeDtypeStruct((t, hw), out_dtype),
                   jax.ShapeDtypeStruct((nb, heads, HGRN_HEAD_DIM, HGRN_HEAD_DIM), F32)],
        scratch_shapes=[pltpu.VMEM((HGRN_HEAD_DIM, HGRN_HEAD_DIM), F32)],
        compiler_params=_params(3),
        name="hgrn",
    )(*args)


def _split_components(q):
    lane = lax.broadcasted_iota(jnp.int32, q.shape, 1)
    zero = jnp.zeros_like(q)
    return jnp.where(lane < DIFF_HEAD_DIM, q, zero), jnp.where(lane >= DIFF_HEAD_DIM, q, zero)


def _online_update(s, vb, m, l, acc):
    m_new = jnp.maximum(m, jnp.max(s, axis=-1, keepdims=True))
    alpha = jnp.exp2(m - m_new)
    p = jnp.exp2(s - m_new)
    l = alpha * l + jnp.sum(p, axis=-1, keepdims=True)
    acc = alpha * acc + jnp.dot(p.astype(BF16), vb, preferred_element_type=F32)
    return m_new, l, acc


def _diff_finish(o0, o1, sc_ref, wsub_ref):
    o = o0 - sc_ref[0] * o1
    return _rms(o, wsub_ref[...]) * sc_ref[1]


def _causal_attn_kernel(sc_ref, q_ref, k_ref, vt_ref, wsub_ref, o_ref, s_a, s_b, *, tile):
    i = pl.program_id(2)
    q_parts = _split_components(q_ref[...])
    vd = DIFF_V_DIM
    kt = tile // 2

    def scores_to(step, buf):
        kb = k_ref[pl.ds(pl.multiple_of(step * kt, kt), kt), :]
        for c in range(2):
            buf[c] = lax.dot_general(kb, q_parts[c], _NT, preferred_element_type=F32)

    def absorb(step, buf, stats, diagonal_half):
        vt = vt_ref[0, 0, step]
        out = []
        for c in range(2):
            m, acc = stats[2 * c], stats[2 * c + 1]
            sc = buf[c]
            if diagonal_half is not None:
                keep = (lax.broadcasted_iota(jnp.int32, (kt, tile), 0) + diagonal_half * kt
                        <= lax.broadcasted_iota(jnp.int32, (kt, tile), 1))
                sc = jnp.where(keep, sc, -jnp.inf)
            m_new = jnp.maximum(m, jnp.max(sc, axis=0, keepdims=True))
            p = jnp.exp2(sc - m_new).astype(BF16)
            acc = jnp.exp2(m - m_new) * acc + jnp.dot(vt, p, preferred_element_type=F32)
            out += [m_new, acc]
        return tuple(out)

    def body(j, stats):
        scores_to(2 * j + 1, s_b)
        stats = absorb(2 * j, s_a, stats, None)
        scores_to(2 * j + 2, s_a)
        return absorb(2 * j + 1, s_b, stats, None)

    neg = jnp.full((1, tile), -jnp.inf, F32)
    zero_acc = jnp.zeros((vd + ONES_ROWS, tile), F32)
    scores_to(0, s_a)
    stats = lax.fori_loop(0, i, body, (neg, zero_acc, neg, zero_acc))
    scores_to(2 * i + 1, s_b)
    stats = absorb(2 * i, s_a, stats, 0)
    _, a0, _, a1 = absorb(2 * i + 1, s_b, stats, 1)
    o = a0[:vd] / a0[vd:vd + 1] - sc_ref[0] * (a1[:vd] / a1[vd:vd + 1])
    o = o * lax.rsqrt(jnp.mean(o * o, axis=0, keepdims=True) + NORM_EPS) * wsub_ref[...] * sc_ref[1]
    o_ref[...] = o.T.astype(o_ref.dtype)


def _causal_attn(qd, kb, vb, scal, wsub, nb, seq):
    t, dw = qd.shape
    heads = dw // DIFF_V_DIM
    tile = min(ATTN_TILE, seq)
    nq = seq // tile
    kt = tile // 2
    nk = seq // kt
    vt = vb.reshape(nb, nk, kt, heads, DIFF_V_DIM).transpose(0, 3, 1, 4, 2)
    vt = jnp.concatenate([vt, jnp.ones((nb, heads, nk, ONES_ROWS, kt), vt.dtype)], axis=3)
    return pl.pallas_call(
        functools.partial(_causal_attn_kernel, tile=tile),
        grid=(nb, heads, nq),
        in_specs=[pl.BlockSpec(memory_space=pltpu.SMEM),
                  pl.BlockSpec((tile, DIFF_V_DIM), lambda b, h, i: (b * nq + i, h)),
                  pl.BlockSpec((seq, DIFF_V_DIM), lambda b, h, i: (b, h)),
                  pl.BlockSpec((1, 1, nk, DIFF_V_DIM + ONES_ROWS, kt), lambda b, h, i: (b, h, 0, 0, 0)),
                  pl.BlockSpec((DIFF_V_DIM, 1), lambda b, h, i: (0, 0))],
        out_specs=pl.BlockSpec((tile, DIFF_V_DIM), lambda b, h, i: (b * nq + i, h)),
        out_shape=jax.ShapeDtypeStruct((t, dw), BF16),
        scratch_shapes=[pltpu.VMEM((2, kt, tile), F32), pltpu.VMEM((2, kt, tile), F32)],
        compiler_params=_params(3),
        name="causal_diff_attn",
    )(scal, qd, kb, vt, wsub.reshape(DIFF_V_DIM, 1))


def _paged_attn_kernel(*refs, heads, dec_seq, n_pages):
    pt_ref, sc_ref, q_ref = refs[0], refs[1], refs[2]
    k_pages = refs[3:3 + n_pages]
    v_pages = refs[3 + n_pages:3 + 2 * n_pages]
    kn_ref, vn_ref, wsub_ref, o_ref, m_ref, l_ref, acc_ref = refs[3 + 2 * n_pages:]
    del pt_ref
    g = pl.program_id(1)
    per_head = 2 * dec_seq
    rows = heads * per_head

    @pl.when(g == 0)
    def _():
        m_ref[...] = jnp.full(m_ref.shape, -jnp.inf, F32)
        l_ref[...] = jnp.zeros(l_ref.shape, F32)
        acc_ref[...] = jnp.zeros(acc_ref.shape, F32)

    q = q_ref[...].astype(BF16)
    parts = []
    for h in range(heads):
        parts.extend(_split_components(q[:, h * DIFF_V_DIM:(h + 1) * DIFF_V_DIM]))
    q_all = jnp.concatenate(parts, axis=0)

    def accumulate(kb, vb, keep):
        s = lax.dot_general(q_all, kb, _NT, preferred_element_type=F32)
        m, l, acc = _online_update(jnp.where(keep, s, -jnp.inf), vb, m_ref[...], l_ref[...], acc_ref[...])
        m_ref[...], l_ref[...], acc_ref[...] = m, l, acc

    n_keys = n_pages * k_pages[0].shape[0]
    col = lax.broadcasted_iota(jnp.int32, (rows, n_keys), 1)
    row = lax.broadcasted_iota(jnp.int32, (rows, n_keys), 0)
    accumulate(jnp.concatenate([kp[...].astype(BF16) for kp in k_pages], axis=0),
               jnp.concatenate([vp[...].astype(BF16) for vp in v_pages], axis=0),
               col % heads == row // per_head)

    @pl.when(g == pl.num_programs(1) - 1)
    def _():
        n_new = kn_ref.shape[1]
        col = lax.broadcasted_iota(jnp.int32, (rows, n_new), 1)
        row = lax.broadcasted_iota(jnp.int32, (rows, n_new), 0)
        keep = (col % heads == row // per_head) & (col // heads <= row % dec_seq)
        accumulate(kn_ref[0].astype(BF16), vn_ref[0].astype(BF16), keep)
        o = acc_ref[...] / l_ref[...]
        for h in range(heads):
            o0 = o[h * per_head:h * per_head + dec_seq]
            o1 = o[h * per_head + dec_seq:(h + 1) * per_head]
            o_ref[:, h * DIFF_V_DIM:(h + 1) * DIFF_V_DIM] = _diff_finish(o0, o1, sc_ref, wsub_ref).astype(o_ref.dtype)


def _paged_attn(qd, k_new, v_new, cache_k2, cache_v2, page_ids, page, scal, wsub, nb, dec_seq):
    t, dw = qd.shape
    heads = dw // DIFF_V_DIM
    page_rows = page * heads
    n_pages = page_ids.shape[1]
    gp = min(PAGES_PER_STEP, n_pages)
    n_steps = n_pages // gp
    new_rows = -(-dec_seq * heads // LANES) * LANES
    pad = lambda a: jnp.pad(a.reshape(nb, dec_seq * heads, DIFF_V_DIM), ((0, 0), (0, new_rows - dec_seq * heads), (0, 0)))
    page_spec = lambda i: pl.BlockSpec((page_rows, DIFF_V_DIM), lambda b, g, pt: (pt[b, g * gp + i], 0))
    tok_spec = pl.BlockSpec((dec_seq, dw), lambda b, g, pt: (b, 0))
    new_spec = pl.BlockSpec((1, new_rows, DIFF_V_DIM), lambda b, g, pt: (b, 0, 0))
    rows = heads * 2 * dec_seq
    grid_spec = pltpu.PrefetchScalarGridSpec(
        num_scalar_prefetch=1,
        grid=(nb, n_steps),
        in_specs=([pl.BlockSpec(memory_space=pltpu.SMEM), tok_spec]
                  + [page_spec(i) for i in range(gp)] * 2
                  + [new_spec, new_spec, pl.BlockSpec((1, DIFF_V_DIM), lambda b, g, pt: (0, 0))]),
        out_specs=tok_spec,
        scratch_shapes=[pltpu.VMEM((rows, 1), F32), pltpu.VMEM((rows, 1), F32), pltpu.VMEM((rows, DIFF_V_DIM), F32)],
    )
    return pl.pallas_call(
        functools.partial(_paged_attn_kernel, heads=heads, dec_seq=dec_seq, n_pages=gp),
        grid_spec=grid_spec,
        out_shape=jax.ShapeDtypeStruct((t, dw), F32),
        compiler_params=_params(2),
        name="paged_diff_attn",
    )(page_ids, scal, qd, *([cache_k2] * gp), *([cache_v2] * gp), pad(k_new), pad(v_new),
      wsub.reshape(1, DIFF_V_DIM))


def _out_proj_kernel(x_ref, oh_ref, od_ref, wo_ref, wn_ref, wr_hi_ref, wr_lo_ref, br_ref,
                     x2_ref, xn_ref, lg_ref, *, hw):
    mix = (jnp.dot(oh_ref[...].astype(BF16), wo_ref[:hw, :], preferred_element_type=F32)
           + jnp.dot(od_ref[...].astype(BF16), wo_ref[hw:, :], preferred_element_type=F32))
    x2 = x_ref[...] + mix
    x2_ref[...] = x2
    xn = _rms(x2, wn_ref[...])
    xn_hi = xn.astype(BF16)
    xn_ref[...] = xn_hi
    xn_lo = (xn - xn_hi.astype(F32)).astype(BF16)
    lg_ref[...] = (jnp.dot(xn_hi, wr_hi_ref[...], preferred_element_type=F32)
                   + jnp.dot(xn_lo, wr_hi_ref[...], preferred_element_type=F32)
                   + jnp.dot(xn_hi, wr_lo_ref[...], preferred_element_type=F32)
                   + br_ref[...])


def _out_proj(x2d, row0, o_h, o_d, wo_bf, wn, w_router, b_router):
    d = x2d.shape[1]
    t, hw = o_h.shape
    dw = o_d.shape[1]
    n_e = w_router.shape[1]
    tm = min(ROW_TILE, t)
    tile0 = row0 // tm
    wr = jnp.pad(w_router.astype(F32), ((0, 0), (0, LANES - n_e)))
    wr_hi = wr.astype(BF16)
    wr_lo = (wr - wr_hi.astype(F32)).astype(BF16)
    br = jnp.pad(b_router.astype(F32), (0, LANES - n_e), constant_values=-jnp.inf).reshape(1, LANES)
    row = lambda i: (i, 0)
    fixed = lambda i: (0, 0)
    return pl.pallas_call(
        functools.partial(_out_proj_kernel, hw=hw),
        grid=(t // tm,),
        in_specs=[pl.BlockSpec((tm, d), lambda i: (tile0 + i, 0)), pl.BlockSpec((tm, hw), row),
                  pl.BlockSpec((tm, dw), row),
                  pl.BlockSpec(wo_bf.shape, fixed), pl.BlockSpec((1, d), fixed),
                  pl.BlockSpec((d, LANES), fixed), pl.BlockSpec((d, LANES), fixed), pl.BlockSpec((1, LANES), fixed)],
        out_specs=[pl.BlockSpec((tm, d), row), pl.BlockSpec((tm, d), row), pl.BlockSpec((tm, LANES), row)],
        out_shape=[jax.ShapeDtypeStruct((t, d), F32), jax.ShapeDtypeStruct((t, d), BF16),
                   jax.ShapeDtypeStruct((t, LANES), F32)],
        compiler_params=_params(1),
        name="out_proj_router",
    )(x2d, o_h, o_d, wo_bf, wn.reshape(1, d), wr_hi, wr_lo, br)


def _router_kernel(lg_ref, gate_ref, idx_ref, rank_ref, cnt_ref, base_ref):
    i = pl.program_id(0)

    @pl.when(i == 0)
    def _():
        base_ref[...] = jnp.zeros(base_ref.shape, F32)

    cur = lg_ref[...]
    tm = cur.shape[0]
    lane = lax.broadcasted_iota(jnp.int32, cur.shape, 1)
    vals, idxs, hits = [], [], []
    for _ in range(TOP_K):
        mx = jnp.max(cur, axis=-1, keepdims=True)
        ix = jnp.min(jnp.where(cur == mx, lane, LANES), axis=-1, keepdims=True)
        hit = lane == ix
        cur = jnp.where(hit, -jnp.inf, cur)
        vals.append(mx)
        idxs.append(ix)
        hits.append(hit)
    es = [jnp.exp(v - vals[0]) for v in vals]
    denom = functools.reduce(lambda a, b: a + b, es)

    chosen = functools.reduce(lambda a, b: a | b, hits)
    chosen_f = jnp.where(chosen, 1.0, 0.0)
    below = (lax.broadcasted_iota(jnp.int32, (tm, tm), 1) < lax.broadcasted_iota(jnp.int32, (tm, tm), 0))
    earlier = base_ref[...] + jnp.dot(jnp.where(below, 1.0, 0.0).astype(BF16), chosen_f.astype(BF16),
                                      preferred_element_type=F32)
    lane_k = lax.broadcasted_iota(jnp.int32, (tm, TOP_K), 1)
    gate = jnp.zeros((tm, TOP_K), F32)
    idx = jnp.zeros((tm, TOP_K), jnp.int32)
    rank = jnp.zeros((tm, TOP_K), jnp.int32)
    for k in range(TOP_K):
        rk = jnp.sum(jnp.where(hits[k], earlier, 0.0), axis=-1, keepdims=True).astype(jnp.int32)
        gate = jnp.where(lane_k == k, es[k] / denom, gate)
        idx = jnp.where(lane_k == k, idxs[k], idx)
        rank = jnp.where(lane_k == k, rk, rank)
    gate_ref[...] = gate
    idx_ref[...] = idx
    rank_ref[...] = rank
    total = base_ref[...] + jnp.sum(chosen_f, axis=0, keepdims=True)
    base_ref[...] = total
    cnt_ref[...] = total.astype(jnp.int32)


def _router(logits):
    t = logits.shape[0]
    tm = min(ROW_TILE, t)
    row = lambda i: (i, 0)
    return pl.pallas_call(
        _router_kernel,
        grid=(t // tm,),
        in_specs=[pl.BlockSpec((tm, LANES), row)],
        out_specs=[pl.BlockSpec((tm, TOP_K), row)] * 3 + [pl.BlockSpec((1, LANES), lambda i: (0, 0))],
        out_shape=[jax.ShapeDtypeStruct((t, TOP_K), F32), jax.ShapeDtypeStruct((t, TOP_K), jnp.int32),
                   jax.ShapeDtypeStruct((t, TOP_K), jnp.int32), jax.ShapeDtypeStruct((1, LANES), jnp.int32)],
        scratch_shapes=[pltpu.VMEM((1, LANES), F32)],
        compiler_params=_params(1),
        name="router_topk",
    )(logits)


def _moe_ffn_kernel(be_ref, nu_ref, x_ref, wg_ref, bg_ref, wu_ref, bu_ref, wd_ref, bd_ref, y_ref,
                    wg_bf, wu_bf, wd_bf):
    i = pl.program_id(0)
    used = i < nu_ref[0]

    @pl.when(used & ((i == 0) | (be_ref[i] != be_ref[jnp.maximum(i - 1, 0)])))
    def _():
        wg_bf[...] = wg_ref[0].astype(BF16)
        wu_bf[...] = wu_ref[0].astype(BF16)
        wd_bf[...] = wd_ref[0].astype(BF16)

    @pl.when(used)
    def _():
        x = x_ref[...]
        g = jnp.dot(x, wg_bf[...], preferred_element_type=F32) + bg_ref[0]
        u = jnp.dot(x, wu_bf[...], preferred_element_type=F32) + bu_ref[0]
        g = jnp.minimum(g, SWIGLU_LIMIT)
        u = jnp.clip(u, -SWIGLU_LIMIT, SWIGLU_LIMIT)
        h = (u + 1.0) * (g * jax.nn.sigmoid(SWIGLU_ALPHA * g))
        y = jnp.dot(h.astype(BF16), wd_bf[...], preferred_element_type=F32) + bd_ref[0]
        y_ref[...] = y.astype(y_ref.dtype)

    @pl.when(jnp.logical_not(used))
    def _():
        y_ref[...] = jnp.zeros(y_ref.shape, y_ref.dtype)


def _moe_ffn(xg, block_e, n_used, wg, bg, wu, bu, wd, bd):
    n_rows, d = xg.shape
    n_e, _, d_ff = wg.shape
    tm = MOE_TILE
    w_spec = lambda k, n: pl.BlockSpec((1, k, n), lambda i, be, nu: (be[i], 0, 0))
    grid_spec = pltpu.PrefetchScalarGridSpec(
        num_scalar_prefetch=2,
        grid=(n_rows // tm,),
        in_specs=[pl.BlockSpec((tm, d), lambda i, be, nu: (i, 0)),
                  w_spec(d, d_ff), w_spec(1, d_ff), w_spec(d, d_ff), w_spec(1, d_ff), w_spec(d_ff, d), w_spec(1, d)],
        out_specs=pl.BlockSpec((tm, d), lambda i, be, nu: (i, 0)),
        scratch_shapes=[pltpu.VMEM((d, d_ff), BF16), pltpu.VMEM((d, d_ff), BF16), pltpu.VMEM((d_ff, d), BF16)],
    )
    return pl.pallas_call(
        _moe_ffn_kernel,
        grid_spec=grid_spec,
        out_shape=jax.ShapeDtypeStruct((n_rows, d), BF16),
        compiler_params=_params(1),
        name="moe_ffn",
    )(block_e, n_used, xg, wg, bg.reshape(n_e, 1, d_ff), wu, bu.reshape(n_e, 1, d_ff), wd, bd.reshape(n_e, 1, d))


def _plan_rows(idx, rank, counts, tm):
    t = idx.shape[0]
    n_e = counts.shape[0]
    n_assign = t * TOP_K
    padded = (counts + tm - 1) // tm * tm
    pad_end = jnp.cumsum(padded)
    start_pad = pad_end - padded
    start_sorted = jnp.cumsum(counts) - counts
    dest = start_pad[idx] + rank
    n_rows = (n_assign + n_e * (tm - 1) + tm - 1) // tm * tm
    n_blocks = n_rows // tm
    block_start = jnp.arange(n_blocks, dtype=jnp.int32) * tm
    block_e = jnp.minimum(jnp.sum((pad_end[None, :] <= block_start[:, None]).astype(jnp.int32), axis=1), n_e - 1)
    tok = jnp.arange(n_assign, dtype=jnp.int32) // TOP_K
    _, sorted_tok = lax.sort((dest.reshape(-1), tok), num_keys=1)
    r = jnp.arange(n_rows, dtype=jnp.int32)
    row_e = jnp.repeat(block_e, tm)
    within = r - start_pad[row_e]
    src = jnp.clip(start_sorted[row_e] + within, 0, n_assign - 1)
    row_tok = jnp.where(within < counts[row_e], sorted_tok[src], 0)
    n_used = (pad_end[-1:] // tm).astype(jnp.int32)
    return dest, row_tok, block_e, n_used


def _combine_kernel(x2_ref, yg_ref, gate_ref, wn_ref, o_ref):
    gate = gate_ref[...]
    y = x2_ref[...]
    for k in range(TOP_K):
        y = y + yg_ref[k].astype(F32) * gate[:, k:k + 1]
    o_ref[...] = _rms(y, wn_ref[...])


def _combine(x2, yg, gate, wn):
    t, d = x2.shape
    tm = min(COMBINE_TILE, t)
    row = lambda i: (i, 0)
    return pl.pallas_call(
        _combine_kernel,
        grid=(t // tm,),
        in_specs=[pl.BlockSpec((tm, d), row), pl.BlockSpec((TOP_K, tm, d), lambda i: (0, i, 0)),
                  pl.BlockSpec((tm, TOP_K), row), pl.BlockSpec((1, d), lambda i: (0, 0))],
        out_specs=pl.BlockSpec((tm, d), row),
        out_shape=jax.ShapeDtypeStruct((t, d), F32),
        compiler_params=_params(1),
        name="moe_combine",
    )(x2, yg, gate, wn.reshape(1, d))


def _mix_and_route(x2d, row0, nb, seq, pos, s0, attend, p):
    t = nb * seq
    hw, dw = p["hw"], p["dw"]
    act_dtype = BF16 if seq % 16 == 0 else F32
    tm = min(ROW_TILE, t)
    tabs = _rope_tables(pos)
    if seq % tm == 0:
        tab_tiles = seq // tm
    else:
        tabs = tuple(jnp.tile(a, (nb, 1)) for a in tabs)
        tab_tiles = t // tm
    qh, f, vi, g, qd, k_rows, kb, v_rows, vb = _in_proj(
        x2d, row0, t, p["w_norm_mix"], p["w_in"], p["lb"], tabs, tab_tiles, act_dtype, hw, dw)
    o_h, s_fin = _hgrn(qh, f, vi, g, p["w_hgrn_norm"], s0, nb, seq, act_dtype)
    o_d = attend(qd, kb, vb, k_rows, v_rows, nb, seq)
    x2, xn, logits = _out_proj(x2d, row0, o_h, o_d, p["w_out"], p["w_norm_ffn"], p["w_router"], p["b_router"])
    n_e = p["w_router"].shape[1]
    gate, idx, rank, counts = _router(logits)
    dest, row_tok, block_e, n_used = _plan_rows(idx, rank, counts[0, :n_e], MOE_TILE)
    return dict(x2=x2, xg=xn[row_tok], gate=gate, dest=dest, block_e=block_e, n_used=n_used,
                k_rows=k_rows, v_rows=v_rows, s_fin=s_fin)


def _experts(st, p):
    yb = _moe_ffn(st["xg"], st["block_e"], st["n_used"], p["w_gate"], p["b_gate"], p["w_up"], p["b_up"],
                  p["w_down"], p["b_down"])
    return yb[st["dest"].T]


def kernel(x_prompt, x_sample, cache_k, cache_v, state_hgrn, page_table, w_norm_mix, w_in, hgrn_lb_logits, w_hgrn_norm, diff_lambda_q1, diff_lambda_k1, diff_lambda_q2, diff_lambda_k2, w_subln, w_out, w_norm_ffn, w_router, b_router, w_gate, b_gate, w_up, b_up, w_down, b_down, w_norm_final):
    depth = w_in.shape[0]
    assert depth == 1, "single-layer trunk only"
    l = 0
    batch, seq, d = x_prompt.shape
    dec_batch, dec_seq, _ = x_sample.shape
    n_pool, page = cache_k.shape[1], cache_k.shape[2]
    past_len = page_table.shape[1] * page
    hw = w_hgrn_norm.shape[1]
    dw = (w_in.shape[2] - 4 * hw) // 3
    heads = dw // DIFF_V_DIM
    lower_bounds = jnp.cumsum(jax.nn.softmax(hgrn_lb_logits.astype(F32), axis=0), axis=0)
    lam_init = 0.8 - 0.6 * math.exp(-0.3 * l)
    lam = (jnp.exp(jnp.sum(diff_lambda_q1[l].astype(F32) * diff_lambda_k1[l].astype(F32)))
           - jnp.exp(jnp.sum(diff_lambda_q2[l].astype(F32) * diff_lambda_k2[l].astype(F32))) + lam_init)
    scal = jnp.stack([lam, jnp.asarray(1.0 - lam_init, F32)]).astype(F32)
    p = dict(hw=hw, dw=dw, lb=lower_bounds[l], w_norm_mix=w_norm_mix[l], w_in=w_in[l].astype(BF16),
             w_hgrn_norm=w_hgrn_norm[l], w_out=w_out[l].astype(BF16), w_norm_ffn=w_norm_ffn[l],
             w_router=w_router[l], b_router=b_router[l],
             w_gate=w_gate[l], b_gate=b_gate[l], w_up=w_up[l], b_up=b_up[l],
             w_down=w_down[l], b_down=b_down[l], w_norm_final=w_norm_final)
    wsub = w_subln[l]

    def attend_prompt(qd, kb, vb, k_rows, v_rows, nb, sq):
        return _causal_attn(qd, kb, vb, scal, wsub, nb, sq)

    cache_k2 = cache_k.reshape(-1, cache_k.shape[-1])
    cache_v2 = cache_v.reshape(-1, cache_v.shape[-1])
    page_ids = page_table.astype(jnp.int32) + l * n_pool

    def attend_sample(qd, kb, vb, k_rows, v_rows, nb, sq):
        return _paged_attn(qd, k_rows, v_rows, cache_k2, cache_v2, page_ids, page, scal, wsub, nb, sq)

    pos_prompt = jnp.arange(seq, dtype=jnp.int32)
    pos_sample = past_len + jnp.arange(dec_seq, dtype=jnp.int32)
    xp2d = x_prompt.reshape(batch * seq, d)
    xs2d = x_sample.reshape(dec_batch * dec_seq, d)

    n_chunks = PROMPT_CHUNKS if batch % PROMPT_CHUNKS == 0 else 1
    cb = batch // n_chunks
    groups = [_mix_and_route(xp2d, c * cb * seq, cb, seq, pos_prompt, None, attend_prompt, p)
              for c in range(n_chunks)]
    groups.append(_mix_and_route(xs2d, 0, dec_batch, dec_seq, pos_sample, state_hgrn[l], attend_sample, p))
    expert_rows = [_experts(st, p) for st in groups]
    ys = [_combine(st["x2"], yg, st["gate"], p["w_norm_final"]) for st, yg in zip(groups, expert_rows)]

    cat = lambda key: jnp.concatenate([st[key] for st in groups[:-1]], axis=0)
    y_p = jnp.concatenate(ys[:-1], axis=0).reshape(batch, seq, d)
    k_p = cat("k_rows").reshape(1, batch, seq, heads, DIFF_V_DIM)
    v_p = cat("v_rows").reshape(1, batch, seq, heads, DIFF_V_DIM)
    s_p = cat("s_fin")[None]
    smp = groups[-1]
    return (y_p, ys[-1].reshape(dec_batch, dec_seq, d), k_p, v_p, s_p,
            smp["k_rows"].reshape(1, dec_batch, dec_seq, heads, DIFF_V_DIM),
            smp["v_rows"].reshape(1, dec_batch, dec_seq, heads, DIFF_V_DIM), smp["s_fin"][None])
```

```python
import functools
import math

import jax
import jax.numpy as jnp
from jax import lax
from jax.experimental import pallas as pl
from jax.experimental.pallas import tpu as pltpu

F32 = jnp.float32
BF16 = jnp.bfloat16

LANES = 128
HGRN_HEAD_DIM = 128
HGRN_CHUNK = 64
DIFF_HEAD_DIM = 64
DIFF_V_DIM = 2 * DIFF_HEAD_DIM
ROT_DIM = DIFF_HEAD_DIM // 4
ROPE_THETA = 500000.0
TOP_K = 4
SWIGLU_LIMIT = 7.0
SWIGLU_ALPHA = 1.702
NORM_EPS = 1e-6
VMEM_LIMIT = 56 * 1024 * 1024

ROW_TILE = 512
ATTN_TILE = 512
HGRN_TILE = 512
MOE_TILE = 256
PAGES_PER_STEP = 8
COMBINE_TILE = 256
ONES_ROWS = 16
QUERY_SCALE = DIFF_HEAD_DIM ** -0.5 * math.log2(math.e)

_NT = (((1,), (1,)), ((), ()))


def _params(n_axes):
    return pltpu.CompilerParams(dimension_semantics=("arbitrary",) * n_axes, vmem_limit_bytes=VMEM_LIMIT)


def _rms(x, w):
    return x * lax.rsqrt(jnp.mean(x * x, axis=-1, keepdims=True) + NORM_EPS) * w


def _in_proj_kernel(x_ref, wn_ref, w_ref, lb_ref, rc_ref, rp_ref, rm_ref,
                    qh_ref, f_ref, vi_ref, g_ref, qd_ref, kr_ref, kb_ref, vr_ref, vb_ref, *, hw, dw):
    h = _rms(x_ref[...], wn_ref[...]).astype(BF16)

    def proj(c0, width):
        return jnp.dot(h, w_ref[:, c0:c0 + width], preferred_element_type=F32)

    hq = proj(0, hw)
    qh_ref[...] = hq * jax.nn.sigmoid(hq)
    lb = lb_ref[...]
    f_ref[...] = lb + (1.0 - lb) * jax.nn.sigmoid(proj(hw, hw))
    vi_ref[...] = proj(2 * hw, hw)
    hg = proj(3 * hw, hw)
    g_ref[...] = hg * jax.nn.sigmoid(hg)

    rc, rp, rm = rc_ref[...], rp_ref[...], rm_ref[...]
    half = ROT_DIM // 2

    def rope(t):
        return t * rc + pltpu.roll(t, half, 1) * rp + pltpu.roll(t, LANES - half, 1) * rm

    dq = proj(4 * hw, dw)
    dk = proj(4 * hw + dw, dw)
    dv = proj(4 * hw + 2 * dw, dw)
    scale = QUERY_SCALE
    heads = dw // LANES
    tm = dq.shape[0]
    for c in range(heads):
        sl = slice(c * LANES, (c + 1) * LANES)
        qd_ref[:, sl] = (rope(dq[:, sl]) * scale).astype(qd_ref.dtype)
        kr = rope(dk[:, sl])
        kr_ref[pl.ds(c, tm, stride=heads), :] = kr
        vr_ref[pl.ds(c, tm, stride=heads), :] = dv[:, sl]
        kb_ref[:, sl] = kr.astype(kb_ref.dtype)
    vb_ref[...] = dv.astype(vb_ref.dtype)


def _in_proj(x2d, row0, t, wn, w_bf, lb, tabs, tab_tiles, act_dtype, hw, dw):
    d = x2d.shape[1]
    tm = min(ROW_TILE, t)
    heads = dw // LANES
    tile0 = row0 // tm
    row = lambda i: (i, 0)
    fixed = lambda i: (0, 0)
    tab = lambda i: (i % tab_tiles, 0)
    act = jax.ShapeDtypeStruct((t, dw), act_dtype)
    cache_rows = jax.ShapeDtypeStruct((t * heads, LANES), F32)
    out_shapes = [jax.ShapeDtypeStruct((t, hw), F32)] * 4 + [act, cache_rows, act, cache_rows, act]
    cache_spec = pl.BlockSpec((tm * heads, LANES), row)
    return pl.pallas_call(
        functools.partial(_in_proj_kernel, hw=hw, dw=dw),
        grid=(t // tm,),
        in_specs=[pl.BlockSpec((tm, d), lambda i: (tile0 + i, 0)),
                  pl.BlockSpec((1, d), fixed),
                  pl.BlockSpec(w_bf.shape, fixed, pipeline_mode=pl.Buffered(1)),
                  pl.BlockSpec((1, hw), fixed),
                  pl.BlockSpec((tm, LANES), tab), pl.BlockSpec((tm, LANES), tab), pl.BlockSpec((tm, LANES), tab)],
        out_specs=([pl.BlockSpec((tm, hw), row)] * 4
                   + [pl.BlockSpec((tm, dw), row), cache_spec, pl.BlockSpec((tm, dw), row), cache_spec,
                      pl.BlockSpec((tm, dw), row)]),
        out_shape=out_shapes,
        compiler_params=_params(1),
        name="in_proj",
    )(x2d, wn.reshape(1, d), w_bf, lb.reshape(1, hw), *tabs)


def _rope_tables(pos):
    half = ROT_DIM // 2
    inv = ROPE_THETA ** (-jnp.arange(0, ROT_DIM, 2, dtype=F32) / ROT_DIM)
    ang = pos.astype(F32)[:, None] * inv[None, :]
    cos, sin = jnp.cos(ang), jnp.sin(ang)
    n = pos.shape[0]
    rest = DIFF_HEAD_DIM - ROT_DIM
    zeros_h = jnp.zeros((n, half), F32)
    rc = jnp.concatenate([cos, cos, jnp.ones((n, rest), F32)], axis=1)
    rp = jnp.concatenate([zeros_h, sin, jnp.zeros((n, rest), F32)], axis=1)
    rm = jnp.concatenate([-sin, zeros_h, jnp.zeros((n, rest), F32)], axis=1)
    reps = LANES // DIFF_HEAD_DIM
    return tuple(jnp.tile(a, (1, reps)) for a in (rc, rp, rm))


def _hgrn_kernel(*refs, has_state):
    if has_state:
        q_ref, f_ref, v_ref, g_ref, wn_ref, s0_ref, o_ref, sfin_ref, st_ref = refs
    else:
        q_ref, f_ref, v_ref, g_ref, wn_ref, o_ref, sfin_ref, st_ref = refs
    j = pl.program_id(2)
    c_len = HGRN_CHUNK
    kdim = HGRN_HEAD_DIM

    @pl.when(j == 0)
    def _():
        st_ref[...] = s0_ref[0, 0].T if has_state else jnp.zeros(st_ref.shape, F32)

    q, f, v = q_ref[...], f_ref[...], v_ref[...]
    rows = q.shape[0]
    if rows < c_len:
        pad = c_len - rows
        q = jnp.concatenate([q, jnp.zeros((pad, kdim), F32)], axis=0)
        v = jnp.concatenate([v, jnp.zeros((pad, kdim), F32)], axis=0)
        f = jnp.concatenate([f, jnp.ones((pad, kdim), F32)], axis=0)
    tl = q.shape[0]
    n_c = tl // c_len

    k = 1.0 - f
    b = jnp.log(f)
    r = lax.broadcasted_iota(jnp.int32, (tl, kdim), 0) % c_len
    s = 1
    while s < c_len:
        b = b + jnp.where(r >= s, pltpu.roll(b, s, 0), 0.0)
        s *= 2
    b3 = b.reshape(n_c, c_len, kdim)
    q3, k3, v3 = (a.reshape(n_c, c_len, kdim) for a in (q, k, v))
    bl = b3[:, c_len - 1:c_len, :]
    qe = (q3 * jnp.exp(b3)).astype(BF16)
    kd = (k3 * jnp.exp(-b3)).astype(BF16)
    kl = (k3 * jnp.exp(bl - b3)).astype(BF16)
    vb = v3.astype(BF16)
    dec = jnp.exp(bl)

    a = jnp.einsum('nck,ndk->ncd', qe, kd, preferred_element_type=F32)
    causal = (lax.broadcasted_iota(jnp.int32, (c_len, c_len), 1)
              <= lax.broadcasted_iota(jnp.int32, (c_len, c_len), 0))
    a = jnp.where(causal[None], a, 0.0).astype(BF16)
    o_intra = jnp.einsum('ncd,ndv->ncv', a, vb, preferred_element_type=F32)
    ds_t = jnp.einsum('ncv,nck->nvk', vb, kl, preferred_element_type=F32)

    st = st_ref[...]
    outs = []
    for c in range(n_c):
        o_inter = lax.dot_general(qe[c], st.astype(BF16), _NT, preferred_element_type=F32)
        outs.append(o_intra[c] + o_inter)
        st = st * dec[c] + ds_t[c]
    st_ref[...] = st
    o = outs[0] if n_c == 1 else jnp.concatenate(outs, axis=0)
    o = o[:rows]
    o_ref[...] = (_rms(o, wn_ref[...]) * g_ref[...]).astype(o_ref.dtype)

    @pl.when(j == pl.num_programs(2) - 1)
    def _():
        sfin_ref[0, 0] = st.T


def _hgrn(qh, f, vi, g, wn, s0, nb, seq, out_dtype):
    t, hw = qh.shape
    heads = hw // HGRN_HEAD_DIM
    tl = min(HGRN_TILE, seq)
    nl = seq // tl
    blk = pl.BlockSpec((tl, HGRN_HEAD_DIM), lambda b, h, j: (b * nl + j, h))
    state_blk = pl.BlockSpec((1, 1, HGRN_HEAD_DIM, HGRN_HEAD_DIM), lambda b, h, j: (b, h, 0, 0))
    in_specs = [blk, blk, blk, blk, pl.BlockSpec((1, HGRN_HEAD_DIM), lambda b, h, j: (0, h))]
    args = [qh, f, vi, g, wn.reshape(1, hw)]
    if s0 is not None:
        in_specs.append(state_blk)
        args.append(s0)
    return pl.pallas_call(
        functools.partial(_hgrn_kernel, has_state=s0 is not None),
        grid=(nb, heads, nl),
        in_specs=in_specs,
        out_specs=[blk, state_blk],
        out_shape=[jax.ShapeDtypeStruct((t, hw), out_dtype),
                   jax.ShapeDtypeStruct((nb, heads, HGRN_HEAD_DIM, HGRN_HEAD_DIM), F32)],
        scratch_shapes=[pltpu.VMEM((HGRN_HEAD_DIM, HGRN_HEAD_DIM), F32)],
        compiler_params=_params(3),
        name="hgrn",
    )(*args)


def _split_components(q):
    lane = lax.broadcasted_iota(jnp.int32, q.shape, 1)
    zero = jnp.zeros_like(q)
    return jnp.where(lane < DIFF_HEAD_DIM, q, zero), jnp.where(lane >= DIFF_HEAD_DIM, q, zero)


def _online_update(s, vb, m, l, acc):
    m_new = jnp.maximum(m, jnp.max(s, axis=-1, keepdims=True))
    alpha = jnp.exp2(m - m_new)
    p = jnp.exp2(s - m_new)
    l = alpha * l + jnp.sum(p, axis=-1, keepdims=True)
    acc = alpha * acc + jnp.dot(p.astype(BF16), vb, preferred_element_type=F32)
    return m_new, l, acc


def _diff_finish(o0, o1, sc_ref, wsub_ref):
    o = o0 - sc_ref[0] * o1
    return _rms(o, wsub_ref[...]) * sc_ref[1]


def _causal_attn_kernel(sc_ref, q_ref, k_ref, vt_ref, wsub_ref, o_ref, s_a, s_b, *, tile):
    i = pl.program_id(2)
    q_parts = _split_components(q_ref[...])
    vd = DIFF_V_DIM
    kt = tile // 2

    def scores_to(step, buf):
        kb = k_ref[pl.ds(pl.multiple_of(step * kt, kt), kt), :]
        for c in range(2):
            buf[c] = lax.dot_general(kb, q_parts[c], _NT, preferred_element_type=F32)

    def absorb(step, buf, stats, diagonal_half):
        vt = vt_ref[0, 0, step]
        out = []
        for c in range(2):
            m, acc = stats[2 * c], stats[2 * c + 1]
            sc = buf[c]
            if diagonal_half is not None:
                keep = (lax.broadcasted_iota(jnp.int32, (kt, tile), 0) + diagonal_half * kt
                        <= lax.broadcasted_iota(jnp.int32, (kt, tile), 1))
                sc = jnp.where(keep, sc, -jnp.inf)
            m_new = jnp.maximum(m, jnp.max(sc, axis=0, keepdims=True))
            p = jnp.exp2(sc - m_new).astype(BF16)
            acc = jnp.exp2(m - m_new) * acc + jnp.dot(vt, p, preferred_element_type=F32)
            out += [m_new, acc]
        return tuple(out)

    def body(j, stats):
        scores_to(2 * j + 1, s_b)
        stats = absorb(2 * j, s_a, stats, None)
        scores_to(2 * j + 2, s_a)
        return absorb(2 * j + 1, s_b, stats, None)

    neg = jnp.full((1, tile), -jnp.inf, F32)
    zero_acc = jnp.zeros((vd + ONES_ROWS, tile), F32)
    scores_to(0, s_a)
    stats = lax.fori_loop(0, i, body, (neg, zero_acc, neg, zero_acc))
    scores_to(2 * i + 1, s_b)
    stats = absorb(2 * i, s_a, stats, 0)
    _, a0, _, a1 = absorb(2 * i + 1, s_b, stats, 1)
    o = a0[:vd] / a0[vd:vd + 1] - sc_ref[0] * (a1[:vd] / a1[vd:vd + 1])
    o = o * lax.rsqrt(jnp.mean(o * o, axis=0, keepdims=True) + NORM_EPS) * wsub_ref[...] * sc_ref[1]
    o_ref[...] = o.T.astype(o_ref.dtype)


def _causal_attn(qd, kb, vb, scal, wsub, nb, seq):
    t, dw = qd.shape
    heads = dw // DIFF_V_DIM
    tile = min(ATTN_TILE, seq)
    nq = seq // tile
    kt = tile // 2
    nk = seq // kt
    vt = vb.reshape(nb, nk, kt, heads, DIFF_V_DIM).transpose(0, 3, 1, 4, 2)
    vt = jnp.concatenate([vt, jnp.ones((nb, heads, nk, ONES_ROWS, kt), vt.dtype)], axis=3)
    return pl.pallas_call(
        functools.partial(_causal_attn_kernel, tile=tile),
        grid=(nb, heads, nq),
        in_specs=[pl.BlockSpec(memory_space=pltpu.SMEM),
                  pl.BlockSpec((tile, DIFF_V_DIM), lambda b, h, i: (b * nq + i, h)),
                  pl.BlockSpec((seq, DIFF_V_DIM), lambda b, h, i: (b, h)),
                  pl.BlockSpec((1, 1, nk, DIFF_V_DIM + ONES_ROWS, kt), lambda b, h, i: (b, h, 0, 0, 0)),
                  pl.BlockSpec((DIFF_V_DIM, 1), lambda b, h, i: (0, 0))],
        out_specs=pl.BlockSpec((tile, DIFF_V_DIM), lambda b, h, i: (b * nq + i, h)),
        out_shape=jax.ShapeDtypeStruct((t, dw), BF16),
        scratch_shapes=[pltpu.VMEM((2, kt, tile), F32), pltpu.VMEM((2, kt, tile), F32)],
        compiler_params=_params(3),
        name="causal_diff_attn",
    )(scal, qd, kb, vt, wsub.reshape(DIFF_V_DIM, 1))


def _paged_attn_kernel(*refs, heads, dec_seq, n_pages):
    pt_ref, sc_ref, q_ref = refs[0], refs[1], refs[2]
    k_pages = refs[3:3 + n_pages]
    v_pages = refs[3 + n_pages:3 + 2 * n_pages]
    kn_ref, vn_ref, wsub_ref, o_ref, m_ref, l_ref, acc_ref = refs[3 + 2 * n_pages:]
    del pt_ref
    g = pl.program_id(1)
    per_head = 2 * dec_seq
    rows = heads * per_head

    @pl.when(g == 0)
    def _():
        m_ref[...] = jnp.full(m_ref.shape, -jnp.inf, F32)
        l_ref[...] = jnp.zeros(l_ref.shape, F32)
        acc_ref[...] = jnp.zeros(acc_ref.shape, F32)

    q = q_ref[...].astype(BF16)
    parts = []
    for h in range(heads):
        parts.extend(_split_components(q[:, h * DIFF_V_DIM:(h + 1) * DIFF_V_DIM]))
    q_all = jnp.concatenate(parts, axis=0)

    def accumulate(kb, vb, keep):
        s = lax.dot_general(q_all, kb, _NT, preferred_element_type=F32)
        m, l, acc = _online_update(jnp.where(keep, s, -jnp.inf), vb, m_ref[...], l_ref[...], acc_ref[...])
        m_ref[...], l_ref[...], acc_ref[...] = m, l, acc

    n_keys = n_pages * k_pages[0].shape[0]
    col = lax.broadcasted_iota(jnp.int32, (rows, n_keys), 1)
    row = lax.broadcasted_iota(jnp.int32, (rows, n_keys), 0)
    accumulate(jnp.concatenate([kp[...].astype(BF16) for kp in k_pages], axis=0),
               jnp.concatenate([vp[...].astype(BF16) for vp in v_pages], axis=0),
               col % heads == row // per_head)

    @pl.when(g == pl.num_programs(1) - 1)
    def _():
        n_new = kn_ref.shape[1]
        col = lax.broadcasted_iota(jnp.int32, (rows, n_new), 1)
        row = lax.broadcasted_iota(jnp.int32, (rows, n_new), 0)
        keep = (col % heads == row // per_head) & (col // heads <= row % dec_seq)
        accumulate(kn_ref[0].astype(BF16), vn_ref[0].astype(BF16), keep)
        o = acc_ref[...] / l_ref[...]
        for h in range(heads):
            o0 = o[h * per_head:h * per_head + dec_seq]
            o1 = o[h * per_head + dec_seq:(h + 1) * per_head]
            o_ref[:, h * DIFF_V_DIM:(h + 1) * DIFF_V_DIM] = _diff_finish(o0, o1, sc_ref, wsub_ref).astype(o_ref.dtype)


def _paged_attn(qd, k_new, v_new, cache_k2, cache_v2, page_ids, page, scal, wsub, nb, dec_seq):
    t, dw = qd.shape
    heads = dw // DIFF_V_DIM
    page_rows = page * heads
    n_pages = page_ids.shape[1]
    gp = min(PAGES_PER_STEP, n_pages)
    n_steps = n_pages // gp
    new_rows = -(-dec_seq * heads // LANES) * LANES
    pad = lambda a: jnp.pad(a.reshape(nb, dec_seq * heads, DIFF_V_DIM), ((0, 0), (0, new_rows - dec_seq * heads), (0, 0)))
    page_spec = lambda i: pl.BlockSpec((page_rows, DIFF_V_DIM), lambda b, g, pt: (pt[b, g * gp + i], 0))
    tok_spec = pl.BlockSpec((dec_seq, dw), lambda b, g, pt: (b, 0))
    new_spec = pl.BlockSpec((1, new_rows, DIFF_V_DIM), lambda b, g, pt: (b, 0, 0))
    rows = heads * 2 * dec_seq
    grid_spec = pltpu.PrefetchScalarGridSpec(
        num_scalar_prefetch=1,
        grid=(nb, n_steps),
        in_specs=([pl.BlockSpec(memory_space=pltpu.SMEM), tok_spec]
                  + [page_spec(i) for i in range(gp)] * 2
                  + [new_spec, new_spec, pl.BlockSpec((1, DIFF_V_DIM), lambda b, g, pt: (0, 0))]),
        out_specs=tok_spec,
        scratch_shapes=[pltpu.VMEM((rows, 1), F32), pltpu.VMEM((rows, 1), F32), pltpu.VMEM((rows, DIFF_V_DIM), F32)],
    )
    return pl.pallas_call(
        functools.partial(_paged_attn_kernel, heads=heads, dec_seq=dec_seq, n_pages=gp),
        grid_spec=grid_spec,
        out_shape=jax.ShapeDtypeStruct((t, dw), F32),
        compiler_params=_params(2),
        name="paged_diff_attn",
    )(page_ids, scal, qd, *([cache_k2] * gp), *([cache_v2] * gp), pad(k_new), pad(v_new),
      wsub.reshape(1, DIFF_V_DIM))


def _out_proj_kernel(x_ref, oh_ref, od_ref, wo_ref, wn_ref, wr_hi_ref, wr_lo_ref, br_ref,
                     x2_ref, xn_ref, lg_ref, *, hw):
    mix = (jnp.dot(oh_ref[...].astype(BF16), wo_ref[:hw, :], preferred_element_type=F32)
           + jnp.dot(od_ref[...].astype(BF16), wo_ref[hw:, :], preferred_element_type=F32))
    x2 = x_ref[...] + mix
    x2_ref[...] = x2
    xn = _rms(x2, wn_ref[...])
    n_sub = xn.shape[1] // LANES
    for s in range(n_sub):
        xn_ref[pl.ds(s, xn.shape[0], stride=n_sub), :] = xn[:, s * LANES:(s + 1) * LANES]
    xn_hi = xn.astype(BF16)
    xn_lo = (xn - xn_hi.astype(F32)).astype(BF16)
    lg_ref[...] = (jnp.dot(xn_hi, wr_hi_ref[...], preferred_element_type=F32)
                   + jnp.dot(xn_lo, wr_hi_ref[...], preferred_element_type=F32)
                   + jnp.dot(xn_hi, wr_lo_ref[...], preferred_element_type=F32)
                   + br_ref[...])


def _out_proj(x2d, row0, o_h, o_d, wo_bf, wn, w_router, b_router):
    d = x2d.shape[1]
    t, hw = o_h.shape
    dw = o_d.shape[1]
    n_e = w_router.shape[1]
    tm = min(ROW_TILE, t)
    tile0 = row0 // tm
    wr = jnp.pad(w_router.astype(F32), ((0, 0), (0, LANES - n_e)))
    wr_hi = wr.astype(BF16)
    wr_lo = (wr - wr_hi.astype(F32)).astype(BF16)
    br = jnp.pad(b_router.astype(F32), (0, LANES - n_e), constant_values=-jnp.inf).reshape(1, LANES)
    row = lambda i: (i, 0)
    fixed = lambda i: (0, 0)
    return pl.pallas_call(
        functools.partial(_out_proj_kernel, hw=hw),
        grid=(t // tm,),
        in_specs=[pl.BlockSpec((tm, d), lambda i: (tile0 + i, 0)), pl.BlockSpec((tm, hw), row),
                  pl.BlockSpec((tm, dw), row),
                  pl.BlockSpec(wo_bf.shape, fixed), pl.BlockSpec((1, d), fixed),
                  pl.BlockSpec((d, LANES), fixed), pl.BlockSpec((d, LANES), fixed), pl.BlockSpec((1, LANES), fixed)],
        out_specs=[pl.BlockSpec((tm, d), row), pl.BlockSpec((tm * (d // LANES), LANES), row),
                   pl.BlockSpec((tm, LANES), row)],
        out_shape=[jax.ShapeDtypeStruct((t, d), F32), jax.ShapeDtypeStruct((t * (d // LANES), LANES), F32),
                   jax.ShapeDtypeStruct((t, LANES), F32)],
        compiler_params=_params(1),
        name="out_proj_router",
    )(x2d, o_h, o_d, wo_bf, wn.reshape(1, d), wr_hi, wr_lo, br)


def _router_kernel(lg_ref, gate_ref, idx_ref, rank_ref, cnt_ref, base_ref):
    i = pl.program_id(0)

    @pl.when(i == 0)
    def _():
        base_ref[...] = jnp.zeros(base_ref.shape, F32)

    cur = lg_ref[...]
    tm = cur.shape[0]
    lane = lax.broadcasted_iota(jnp.int32, cur.shape, 1)
    vals, idxs, hits = [], [], []
    for _ in range(TOP_K):
        mx = jnp.max(cur, axis=-1, keepdims=True)
        ix = jnp.min(jnp.where(cur == mx, lane, LANES), axis=-1, keepdims=True)
        hit = lane == ix
        cur = jnp.where(hit, -jnp.inf, cur)
        vals.append(mx)
        idxs.append(ix)
        hits.append(hit)
    es = [jnp.exp(v - vals[0]) for v in vals]
    denom = functools.reduce(lambda a, b: a + b, es)

    chosen = functools.reduce(lambda a, b: a | b, hits)
    chosen_f = jnp.where(chosen, 1.0, 0.0)
    below = (lax.broadcasted_iota(jnp.int32, (tm, tm), 1) < lax.broadcasted_iota(jnp.int32, (tm, tm), 0))
    earlier = base_ref[...] + jnp.dot(jnp.where(below, 1.0, 0.0).astype(BF16), chosen_f.astype(BF16),
                                      preferred_element_type=F32)
    lane_k = lax.broadcasted_iota(jnp.int32, (tm, TOP_K), 1)
    gate = jnp.zeros((tm, TOP_K), F32)
    idx = jnp.zeros((tm, TOP_K), jnp.int32)
    rank = jnp.zeros((tm, TOP_K), jnp.int32)
    for k in range(TOP_K):
        rk = jnp.sum(jnp.where(hits[k], earlier, 0.0), axis=-1, keepdims=True).astype(jnp.int32)
        gate = jnp.where(lane_k == k, es[k] / denom, gate)
        idx = jnp.where(lane_k == k, idxs[k], idx)
        rank = jnp.where(lane_k == k, rk, rank)
    gate_ref[...] = gate
    idx_ref[...] = idx
    rank_ref[...] = rank
    total = base_ref[...] + jnp.sum(chosen_f, axis=0, keepdims=True)
    base_ref[...] = total
    cnt_ref[...] = total.astype(jnp.int32)


def _router(logits):
    t = logits.shape[0]
    tm = min(ROW_TILE, t)
    row = lambda i: (i, 0)
    return pl.pallas_call(
        _router_kernel,
        grid=(t // tm,),
        in_specs=[pl.BlockSpec((tm, LANES), row)],
        out_specs=[pl.BlockSpec((tm, TOP_K), row)] * 3 + [pl.BlockSpec((1, LANES), lambda i: (0, 0))],
        out_shape=[jax.ShapeDtypeStruct((t, TOP_K), F32), jax.ShapeDtypeStruct((t, TOP_K), jnp.int32),
                   jax.ShapeDtypeStruct((t, TOP_K), jnp.int32), jax.ShapeDtypeStruct((1, LANES), jnp.int32)],
        scratch_shapes=[pltpu.VMEM((1, LANES), F32)],
        compiler_params=_params(1),
        name="router_topk",
    )(logits)


def _moe_ffn_kernel(be_ref, src0_ref, src_ref, dstp_ref, dstl_ref, x_hbm, wg_ref, bg_ref, wu_ref, bu_ref,
                    wd_ref, bd_ref, y_hbm, wg_bf, wu_bf, wd_bf, xbuf, xs, ybuf, sem_g, sem_s, *, tm, spare0):
    i = pl.program_id(0)
    last = pl.num_programs(0) - 1
    cur = lax.rem(i, 2)
    prev = 1 - cur
    n_sub = xbuf.shape[1] // tm

    def token_copy(idx_ref, r, slot, to_vmem):
        hbm_rows = pl.ds(pl.multiple_of(idx_ref[0, 0, r], n_sub), n_sub)
        vmem_rows = pl.ds(r * n_sub, n_sub)
        if to_vmem:
            return pltpu.make_async_copy(x_hbm.at[hbm_rows], xbuf.at[slot, vmem_rows], sem_g.at[slot])
        return pltpu.make_async_copy(ybuf.at[slot, vmem_rows], y_hbm.at[hbm_rows], sem_s.at[slot])

    def gather_rows(idx_ref, slot):
        for r in range(tm):
            token_copy(idx_ref, r, slot, True).start()

    def scatter_rows(idx_ref, slot):
        for r in range(tm):
            token_copy(idx_ref, r, slot, False).start()

    def wait_gather(slot):
        pltpu.make_async_copy(x_hbm.at[pl.ds(0, tm * n_sub)], xbuf.at[slot], sem_g.at[slot]).wait()

    def wait_scatter(slot):
        pltpu.make_async_copy(ybuf.at[slot], y_hbm.at[pl.ds(0, tm * n_sub)], sem_s.at[slot]).wait()

    @pl.when(i == 0)
    def _():
        gather_rows(src0_ref, 0)
        ybuf[...] = jnp.zeros(ybuf.shape, F32)
        pltpu.make_async_copy(ybuf.at[0], y_hbm.at[pl.ds(spare0, tm * n_sub)], sem_s.at[0]).start()

    @pl.when((i == 0) | (be_ref[i] != be_ref[jnp.maximum(i - 1, 0)]))
    def _():
        wg_bf[...] = wg_ref[0].astype(BF16)
        wu_bf[...] = wu_ref[0].astype(BF16)
        wd_bf[...] = wd_ref[0].astype(BF16)

    wait_gather(cur)
    for s in range(n_sub):
        xs[:, s * LANES:(s + 1) * LANES] = xbuf[cur, pl.ds(s, tm, stride=n_sub), :].astype(BF16)
    gather_rows(src_ref, prev)
    scatter_rows(dstp_ref, prev)
    x = xs[...]
    g = jnp.dot(x, wg_bf[...], preferred_element_type=F32) + bg_ref[0]
    u = jnp.dot(x, wu_bf[...], preferred_element_type=F32) + bu_ref[0]
    g = jnp.minimum(g, SWIGLU_LIMIT)
    u = jnp.clip(u, -SWIGLU_LIMIT, SWIGLU_LIMIT)
    h = (u + 1.0) * (g * jax.nn.sigmoid(SWIGLU_ALPHA * g))
    y = jnp.dot(h.astype(BF16), wd_bf[...], preferred_element_type=F32) + bd_ref[0]
    wait_scatter(cur)
    for s in range(n_sub):
        ybuf[cur, pl.ds(s, tm, stride=n_sub), :] = y[:, s * LANES:(s + 1) * LANES]

    @pl.when(i == last)
    def _():
        scatter_rows(dstl_ref, cur)
        wait_scatter(cur)
        wait_scatter(prev)
        wait_gather(prev)


def _moe_ffn(xn_rows, plan, n_out_rows, wg, bg, wu, bu, wd, bd):
    d = wg.shape[1]
    n_sub = d // LANES
    n_e, _, d_ff = wg.shape
    tm = MOE_TILE
    block_e, src_blocks, dst_blocks = plan["block_e"], plan["src_blocks"], plan["dst_blocks"]
    n_blocks = block_e.shape[0]
    w_spec = lambda k, n: pl.BlockSpec((1, k, n), lambda i, be: (be[i], 0, 0))
    idx_spec = lambda f: pl.BlockSpec((1, 1, tm), lambda i, be: (f(i), 0, 0), memory_space=pltpu.SMEM)
    grid_spec = pltpu.PrefetchScalarGridSpec(
        num_scalar_prefetch=1,
        grid=(n_blocks,),
        in_specs=[idx_spec(lambda i: 0), idx_spec(lambda i: jnp.minimum(i + 1, n_blocks - 1)),
                  idx_spec(lambda i: i), idx_spec(lambda i: n_blocks),
                  pl.BlockSpec(memory_space=pl.ANY),
                  w_spec(d, d_ff), w_spec(1, d_ff), w_spec(d, d_ff), w_spec(1, d_ff), w_spec(d_ff, d), w_spec(1, d)],
        out_specs=pl.BlockSpec(memory_space=pl.ANY),
        scratch_shapes=[pltpu.VMEM((d, d_ff), BF16), pltpu.VMEM((d, d_ff), BF16), pltpu.VMEM((d_ff, d), BF16),
                        pltpu.VMEM((2, tm * n_sub, LANES), F32), pltpu.VMEM((tm, d), BF16),
                        pltpu.VMEM((2, tm * n_sub, LANES), F32),
                        pltpu.SemaphoreType.DMA((2,)), pltpu.SemaphoreType.DMA((2,))],
    )
    return pl.pallas_call(
        functools.partial(_moe_ffn_kernel, tm=tm, spare0=plan["spare0"]),
        grid_spec=grid_spec,
        out_shape=jax.ShapeDtypeStruct((n_out_rows * n_sub, LANES), F32),
        compiler_params=_params(1),
        name="moe_ffn",
    )(block_e, src_blocks, src_blocks, dst_blocks, dst_blocks, xn_rows,
      wg, bg.reshape(n_e, 1, d_ff), wu, bu.reshape(n_e, 1, d_ff), wd, bd.reshape(n_e, 1, d))


def _plan_rows(idx, rank, counts, tm, n_sub):
    t = idx.shape[0]
    n_e = counts.shape[0]
    n_assign = t * TOP_K
    padded = (counts + tm - 1) // tm * tm
    pad_end = jnp.cumsum(padded)
    start_pad = pad_end - padded
    start_sorted = jnp.cumsum(counts) - counts
    dest = start_pad[idx] + rank
    n_rows = (n_assign + n_e * (tm - 1) + tm - 1) // tm * tm
    n_blocks = n_rows // tm
    block_start = jnp.arange(n_blocks, dtype=jnp.int32) * tm
    block_e = jnp.minimum(jnp.sum((pad_end[None, :] <= block_start[:, None]).astype(jnp.int32), axis=1), n_e - 1)
    assign = jnp.arange(n_assign, dtype=jnp.int32)
    _, sorted_assign = lax.sort((dest.reshape(-1), assign), num_keys=1)
    r = jnp.arange(n_rows, dtype=jnp.int32)
    row_e = jnp.repeat(block_e, tm)
    within = r - start_pad[row_e]
    valid = within < counts[row_e]
    a = sorted_assign[jnp.clip(start_sorted[row_e] + within, 0, n_assign - 1)]
    spare0 = n_assign
    row_src = jnp.where(valid, a // TOP_K, 0)
    row_dst = jnp.where(valid, (a % TOP_K) * t + a // TOP_K, spare0 + (2 + (r // tm) % 2) * tm + r % tm)
    before_first = spare0 + tm + jnp.arange(tm, dtype=jnp.int32)
    dst_blocks = jnp.concatenate([before_first, row_dst]).reshape(n_blocks + 1, 1, tm)
    plan = dict(block_e=block_e.astype(jnp.int32), src_blocks=row_src.reshape(n_blocks, 1, tm) * n_sub,
                dst_blocks=dst_blocks * n_sub, spare0=spare0 * n_sub)
    return plan, n_assign + 4 * tm


def _combine_kernel(x2_ref, *refs):
    y_refs, (gate_ref, wn_ref, o_ref) = refs[:TOP_K], refs[TOP_K:]
    gate = gate_ref[...]
    y = x2_ref[...]
    tm, d = y.shape
    n_sub = d // LANES
    for k in range(TOP_K):
        rows = jnp.concatenate([y_refs[k][pl.ds(s, tm, stride=n_sub), :] for s in range(n_sub)], axis=1)
        y = y + rows * gate[:, k:k + 1]
    o_ref[...] = _rms(y, wn_ref[...])


def _combine(x2, y_rows, gate, wn):
    t, d = x2.shape
    tm = min(COMBINE_TILE, t)
    n_sub = d // LANES
    row = lambda i: (i, 0)
    y_spec = lambda k: pl.BlockSpec((tm * n_sub, LANES), lambda i: (k * (t // tm) + i, 0))
    return pl.pallas_call(
        _combine_kernel,
        grid=(t // tm,),
        in_specs=([pl.BlockSpec((tm, d), row)] + [y_spec(k) for k in range(TOP_K)]
                  + [pl.BlockSpec((tm, TOP_K), row), pl.BlockSpec((1, d), lambda i: (0, 0))]),
        out_specs=pl.BlockSpec((tm, d), row),
        out_shape=jax.ShapeDtypeStruct((t, d), F32),
        compiler_params=_params(1),
        name="moe_combine",
    )(x2, *([y_rows] * TOP_K), gate, wn.reshape(1, d))


def _mix_and_route(x2d, row0, nb, seq, pos, s0, attend, p):
    t = nb * seq
    hw, dw = p["hw"], p["dw"]
    act_dtype = BF16 if seq % 16 == 0 else F32
    tm = min(ROW_TILE, t)
    tabs = _rope_tables(pos)
    if seq % tm == 0:
        tab_tiles = seq // tm
    else:
        tabs = tuple(jnp.tile(a, (nb, 1)) for a in tabs)
        tab_tiles = t // tm
    qh, f, vi, g, qd, k_rows, kb, v_rows, vb = _in_proj(
        x2d, row0, t, p["w_norm_mix"], p["w_in"], p["lb"], tabs, tab_tiles, act_dtype, hw, dw)
    o_h, s_fin = _hgrn(qh, f, vi, g, p["w_hgrn_norm"], s0, nb, seq, act_dtype)
    o_d = attend(qd, kb, vb, k_rows, v_rows, nb, seq)
    x2, xn, logits = _out_proj(x2d, row0, o_h, o_d, p["w_out"], p["w_norm_ffn"], p["w_router"], p["b_router"])
    n_e = p["w_router"].shape[1]
    gate, idx, rank, counts = _router(logits)
    plan, n_out_rows = _plan_rows(idx, rank, counts[0, :n_e], MOE_TILE, x2.shape[1] // LANES)
    return dict(x2=x2, xn=xn, gate=gate, plan=plan, n_out_rows=n_out_rows,
                k_rows=k_rows, v_rows=v_rows, s_fin=s_fin)


def _experts(st, p):
    return _moe_ffn(st["xn"], st["plan"], st["n_out_rows"], p["w_gate"], p["b_gate"], p["w_up"], p["b_up"],
                    p["w_down"], p["b_down"])


def kernel(x_prompt, x_sample, cache_k, cache_v, state_hgrn, page_table, w_norm_mix, w_in, hgrn_lb_logits, w_hgrn_norm, diff_lambda_q1, diff_lambda_k1, diff_lambda_q2, diff_lambda_k2, w_subln, w_out, w_norm_ffn, w_router, b_router, w_gate, b_gate, w_up, b_up, w_down, b_down, w_norm_final):
    depth = w_in.shape[0]
    assert depth == 1, "single-layer trunk only"
    l = 0
    batch, seq, d = x_prompt.shape
    dec_batch, dec_seq, _ = x_sample.shape
    n_pool, page = cache_k.shape[1], cache_k.shape[2]
    past_len = page_table.shape[1] * page
    hw = w_hgrn_norm.shape[1]
    dw = (w_in.shape[2] - 4 * hw) // 3
    heads = dw // DIFF_V_DIM
    lower_bounds = jnp.cumsum(jax.nn.softmax(hgrn_lb_logits.astype(F32), axis=0), axis=0)
    lam_init = 0.8 - 0.6 * math.exp(-0.3 * l)
    lam = (jnp.exp(jnp.sum(diff_lambda_q1[l].astype(F32) * diff_lambda_k1[l].astype(F32)))
           - jnp.exp(jnp.sum(diff_lambda_q2[l].astype(F32) * diff_lambda_k2[l].astype(F32))) + lam_init)
    scal = jnp.stack([lam, jnp.asarray(1.0 - lam_init, F32)]).astype(F32)
    p = dict(hw=hw, dw=dw, lb=lower_bounds[l], w_norm_mix=w_norm_mix[l], w_in=w_in[l].astype(BF16),
             w_hgrn_norm=w_hgrn_norm[l], w_out=w_out[l].astype(BF16), w_norm_ffn=w_norm_ffn[l],
             w_router=w_router[l], b_router=b_router[l],
             w_gate=w_gate[l], b_gate=b_gate[l], w_up=w_up[l], b_up=b_up[l],
             w_down=w_down[l], b_down=b_down[l], w_norm_final=w_norm_final)
    wsub = w_subln[l]

    def attend_prompt(qd, kb, vb, k_rows, v_rows, nb, sq):
        return _causal_attn(qd, kb, vb, scal, wsub, nb, sq)

    cache_k2 = cache_k.reshape(-1, cache_k.shape[-1])
    cache_v2 = cache_v.reshape(-1, cache_v.shape[-1])
    page_ids = page_table.astype(jnp.int32) + l * n_pool

    def attend_sample(qd, kb, vb, k_rows, v_rows, nb, sq):
        return _paged_attn(qd, k_rows, v_rows, cache_k2, cache_v2, page_ids, page, scal, wsub, nb, sq)

    pos_prompt = jnp.arange(seq, dtype=jnp.int32)
    pos_sample = past_len + jnp.arange(dec_seq, dtype=jnp.int32)
    xp2d = x_prompt.reshape(batch * seq, d)
    xs2d = x_sample.reshape(dec_batch * dec_seq, d)

    prm = _mix_and_route(xp2d, 0, batch, seq, pos_prompt, None, attend_prompt, p)
    smp = _mix_and_route(xs2d, 0, dec_batch, dec_seq, pos_sample, state_hgrn[l], attend_sample, p)
    y_p, y_s = (_combine(st["x2"], _experts(st, p), st["gate"], p["w_norm_final"]) for st in (prm, smp))
    cache_rows = lambda a, nb, sq: a.reshape(1, nb, sq, heads, DIFF_V_DIM)
    return (y_p.reshape(batch, seq, d), y_s.reshape(dec_batch, dec_seq, d),
            cache_rows(prm["k_rows"], batch, seq), cache_rows(prm["v_rows"], batch, seq), prm["s_fin"][None],
            cache_rows(smp["k_rows"], dec_batch, dec_seq), cache_rows(smp["v_rows"], dec_batch, dec_seq),
            smp["s_fin"][None])
```

```python
import functools
import math

import jax
import jax.numpy as jnp
from jax import lax
from jax.experimental import pallas as pl
from jax.experimental.pallas import tpu as pltpu

F32 = jnp.float32
BF16 = jnp.bfloat16

LANES = 128
HGRN_HEAD_DIM = 128
HGRN_CHUNK = 64
DIFF_HEAD_DIM = 64
DIFF_V_DIM = 2 * DIFF_HEAD_DIM
ROT_DIM = DIFF_HEAD_DIM // 4
ROPE_THETA = 500000.0
TOP_K = 4
SWIGLU_LIMIT = 7.0
SWIGLU_ALPHA = 1.702
NORM_EPS = 1e-6
VMEM_LIMIT = 56 * 1024 * 1024

ROW_TILE = 512
ATTN_TILE = 512
HGRN_TILE = 512
MOE_TILE = 256
PAGES_PER_STEP = 8
COMBINE_TILE = 256
ONES_ROWS = 16
QUERY_SCALE = DIFF_HEAD_DIM ** -0.5 * math.log2(math.e)

_NT = (((1,), (1,)), ((), ()))


def _params(n_axes):
    return pltpu.CompilerParams(dimension_semantics=("arbitrary",) * n_axes, vmem_limit_bytes=VMEM_LIMIT)


def _rms(x, w):
    return x * lax.rsqrt(jnp.mean(x * x, axis=-1, keepdims=True) + NORM_EPS) * w


def _in_proj_kernel(x_ref, wn_ref, w_ref, lb_ref, rc_ref, rp_ref, rm_ref,
                    qh_ref, f_ref, vi_ref, g_ref, qd_ref, kr_ref, kb_ref, vr_ref, vb_ref, *, hw, dw):
    h = _rms(x_ref[...], wn_ref[...]).astype(BF16)

    def proj(c0, width):
        return jnp.dot(h, w_ref[:, c0:c0 + width], preferred_element_type=F32)

    hq = proj(0, hw)
    qh_ref[...] = hq * jax.nn.sigmoid(hq)
    lb = lb_ref[...]
    f_ref[...] = lb + (1.0 - lb) * jax.nn.sigmoid(proj(hw, hw))
    vi_ref[...] = proj(2 * hw, hw)
    hg = proj(3 * hw, hw)
    g_ref[...] = hg * jax.nn.sigmoid(hg)

    rc, rp, rm = rc_ref[...], rp_ref[...], rm_ref[...]
    half = ROT_DIM // 2

    def rope(t):
        return t * rc + pltpu.roll(t, half, 1) * rp + pltpu.roll(t, LANES - half, 1) * rm

    dq = proj(4 * hw, dw)
    dk = proj(4 * hw + dw, dw)
    dv = proj(4 * hw + 2 * dw, dw)
    scale = QUERY_SCALE
    heads = dw // LANES
    tm = dq.shape[0]
    for c in range(heads):
        sl = slice(c * LANES, (c + 1) * LANES)
        qd_ref[:, sl] = (rope(dq[:, sl]) * scale).astype(qd_ref.dtype)
        kr = rope(dk[:, sl])
        kr_ref[pl.ds(c, tm, stride=heads), :] = kr
        vr_ref[pl.ds(c, tm, stride=heads), :] = dv[:, sl]
        kb_ref[:, sl] = kr.astype(kb_ref.dtype)
    vb_ref[...] = dv.astype(vb_ref.dtype)


def _in_proj(x2d, row0, t, wn, w_bf, lb, tabs, tab_tiles, act_dtype, hw, dw):
    d = x2d.shape[1]
    tm = min(ROW_TILE, t)
    heads = dw // LANES
    tile0 = row0 // tm
    row = lambda i: (i, 0)
    fixed = lambda i: (0, 0)
    tab = lambda i: (i % tab_tiles, 0)
    act = jax.ShapeDtypeStruct((t, dw), act_dtype)
    cache_rows = jax.ShapeDtypeStruct((t * heads, LANES), F32)
    out_shapes = [jax.ShapeDtypeStruct((t, hw), F32)] * 4 + [act, cache_rows, act, cache_rows, act]
    cache_spec = pl.BlockSpec((tm * heads, LANES), row)
    return pl.pallas_call(
        functools.partial(_in_proj_kernel, hw=hw, dw=dw),
        grid=(t // tm,),
        in_specs=[pl.BlockSpec((tm, d), lambda i: (tile0 + i, 0)),
                  pl.BlockSpec((1, d), fixed),
                  pl.BlockSpec(w_bf.shape, fixed, pipeline_mode=pl.Buffered(1)),
                  pl.BlockSpec((1, hw), fixed),
                  pl.BlockSpec((tm, LANES), tab), pl.BlockSpec((tm, LANES), tab), pl.BlockSpec((tm, LANES), tab)],
        out_specs=([pl.BlockSpec((tm, hw), row)] * 4
                   + [pl.BlockSpec((tm, dw), row), cache_spec, pl.BlockSpec((tm, dw), row), cache_spec,
                      pl.BlockSpec((tm, dw), row)]),
        out_shape=out_shapes,
        compiler_params=_params(1),
        name="in_proj",
    )(x2d, wn.reshape(1, d), w_bf, lb.reshape(1, hw), *tabs)


def _rope_tables(pos):
    half = ROT_DIM // 2
    inv = ROPE_THETA ** (-jnp.arange(0, ROT_DIM, 2, dtype=F32) / ROT_DIM)
    ang = pos.astype(F32)[:, None] * inv[None, :]
    cos, sin = jnp.cos(ang), jnp.sin(ang)
    n = pos.shape[0]
    rest = DIFF_HEAD_DIM - ROT_DIM
    zeros_h = jnp.zeros((n, half), F32)
    rc = jnp.concatenate([cos, cos, jnp.ones((n, rest), F32)], axis=1)
    rp = jnp.concatenate([zeros_h, sin, jnp.zeros((n, rest), F32)], axis=1)
    rm = jnp.concatenate([-sin, zeros_h, jnp.zeros((n, rest), F32)], axis=1)
    reps = LANES // DIFF_HEAD_DIM
    return tuple(jnp.tile(a, (1, reps)) for a in (rc, rp, rm))


def _hgrn_kernel(*refs, has_state):
    if has_state:
        q_ref, f_ref, v_ref, g_ref, wn_ref, s0_ref, o_ref, sfin_ref, st_ref = refs
    else:
        q_ref, f_ref, v_ref, g_ref, wn_ref, o_ref, sfin_ref, st_ref = refs
    j = pl.program_id(2)
    c_len = HGRN_CHUNK
    kdim = HGRN_HEAD_DIM

    @pl.when(j == 0)
    def _():
        st_ref[...] = s0_ref[0, 0].T if has_state else jnp.zeros(st_ref.shape, F32)

    q, f, v = q_ref[...], f_ref[...], v_ref[...]
    rows = q.shape[0]
    if rows < c_len:
        pad = c_len - rows
        q = jnp.concatenate([q, jnp.zeros((pad, kdim), F32)], axis=0)
        v = jnp.concatenate([v, jnp.zeros((pad, kdim), F32)], axis=0)
        f = jnp.concatenate([f, jnp.ones((pad, kdim), F32)], axis=0)
    tl = q.shape[0]
    n_c = tl // c_len

    k = 1.0 - f
    b = jnp.log(f)
    r = lax.broadcasted_iota(jnp.int32, (tl, kdim), 0) % c_len
    s = 1
    while s < c_len:
        b = b + jnp.where(r >= s, pltpu.roll(b, s, 0), 0.0)
        s *= 2
    b3 = b.reshape(n_c, c_len, kdim)
    q3, k3, v3 = (a.reshape(n_c, c_len, kdim) for a in (q, k, v))
    bl = b3[:, c_len - 1:c_len, :]
    qe = (q3 * jnp.exp(b3)).astype(BF16)
    kd = (k3 * jnp.exp(-b3)).astype(BF16)
    kl = (k3 * jnp.exp(bl - b3)).astype(BF16)
    vb = v3.astype(BF16)
    dec = jnp.exp(bl)

    a = jnp.einsum('nck,ndk->ncd', qe, kd, preferred_element_type=F32)
    causal = (lax.broadcasted_iota(jnp.int32, (c_len, c_len), 1)
              <= lax.broadcasted_iota(jnp.int32, (c_len, c_len), 0))
    a = jnp.where(causal[None], a, 0.0).astype(BF16)
    o_intra = jnp.einsum('ncd,ndv->ncv', a, vb, preferred_element_type=F32)
    ds_t = jnp.einsum('ncv,nck->nvk', vb, kl, preferred_element_type=F32)

    st = st_ref[...]
    outs = []
    for c in range(n_c):
        o_inter = lax.dot_general(qe[c], st.astype(BF16), _NT, preferred_element_type=F32)
        outs.append(o_intra[c] + o_inter)
        st = st * dec[c] + ds_t[c]
    st_ref[...] = st
    o = outs[0] if n_c == 1 else jnp.concatenate(outs, axis=0)
    o = o[:rows]
    o_ref[...] = (_rms(o, wn_ref[...]) * g_ref[...]).astype(o_ref.dtype)

    @pl.when(j == pl.num_programs(2) - 1)
    def _():
        sfin_ref[0, 0] = st.T


def _hgrn(qh, f, vi, g, wn, s0, nb, seq, out_dtype):
    t, hw = qh.shape
    heads = hw // HGRN_HEAD_DIM
    tl = min(HGRN_TILE, seq)
    nl = seq // tl
    blk = pl.BlockSpec((tl, HGRN_HEAD_DIM), lambda b, h, j: (b * nl + j, h))
    state_blk = pl.BlockSpec((1, 1, HGRN_HEAD_DIM, HGRN_HEAD_DIM), lambda b, h, j: (b, h, 0, 0))
    in_specs = [blk, blk, blk, blk, pl.BlockSpec((1, HGRN_HEAD_DIM), lambda b, h, j: (0, h))]
    args = [qh, f, vi, g, wn.reshape(1, hw)]
    if s0 is not None:
        in_specs.append(state_blk)
        args.append(s0)
    return pl.pallas_call(
        functools.partial(_hgrn_kernel, has_state=s0 is not None),
        grid=(nb, heads, nl),
        in_specs=in_specs,
        out_specs=[blk, state_blk],
        out_shape=[jax.ShapeDtypeStruct((t, hw), out_dtype),
                   jax.ShapeDtypeStruct((nb, heads, HGRN_HEAD_DIM, HGRN_HEAD_DIM), F32)],
        scratch_shapes=[pltpu.VMEM((HGRN_HEAD_DIM, HGRN_HEAD_DIM), F32)],
        compiler_params=_params(3),
        name="hgrn",
    )(*args)


def _split_components(q):
    lane = lax.broadcasted_iota(jnp.int32, q.shape, 1)
    zero = jnp.zeros_like(q)
    return jnp.where(lane < DIFF_HEAD_DIM, q, zero), jnp.where(lane >= DIFF_HEAD_DIM, q, zero)


def _online_update(s, vb, m, l, acc):
    m_new = jnp.maximum(m, jnp.max(s, axis=-1, keepdims=True))
    alpha = jnp.exp2(m - m_new)
    p = jnp.exp2(s - m_new)
    l = alpha * l + jnp.sum(p, axis=-1, keepdims=True)
    acc = alpha * acc + jnp.dot(p.astype(BF16), vb, preferred_element_type=F32)
    return m_new, l, acc


def _diff_finish(o0, o1, sc_ref, wsub_ref):
    o = o0 - sc_ref[0] * o1
    return _rms(o, wsub_ref[...]) * sc_ref[1]


def _causal_attn_kernel(sc_ref, q_ref, k_ref, vt_ref, wsub_ref, o_ref, s_a, s_b, *, tile):
    i = pl.program_id(2)
    q_parts = _split_components(q_ref[...])
    vd = DIFF_V_DIM
    kt = tile // 2

    def scores_to(step, buf):
        kb = k_ref[pl.ds(pl.multiple_of(step * kt, kt), kt), :]
        for c in range(2):
            buf[c] = lax.dot_general(kb, q_parts[c], _NT, preferred_element_type=F32)

    def absorb(step, buf, stats, diagonal_half):
        vt = vt_ref[0, 0, step]
        out = []
        for c in range(2):
            m, acc = stats[2 * c], stats[2 * c + 1]
            sc = buf[c]
            if diagonal_half is not None:
                keep = (lax.broadcasted_iota(jnp.int32, (kt, tile), 0) + diagonal_half * kt
                        <= lax.broadcasted_iota(jnp.int32, (kt, tile), 1))
                sc = jnp.where(keep, sc, -jnp.inf)
            m_new = jnp.maximum(m, jnp.max(sc, axis=0, keepdims=True))
            p = jnp.exp2(sc - m_new).astype(BF16)
            acc = jnp.exp2(m - m_new) * acc + jnp.dot(vt, p, preferred_element_type=F32)
            out += [m_new, acc]
        return tuple(out)

    def body(j, stats):
        scores_to(2 * j + 1, s_b)
        stats = absorb(2 * j, s_a, stats, None)
        scores_to(2 * j + 2, s_a)
        return absorb(2 * j + 1, s_b, stats, None)

    neg = jnp.full((1, tile), -jnp.inf, F32)
    zero_acc = jnp.zeros((vd + ONES_ROWS, tile), F32)
    scores_to(0, s_a)
    stats = lax.fori_loop(0, i, body, (neg, zero_acc, neg, zero_acc))
    scores_to(2 * i + 1, s_b)
    stats = absorb(2 * i, s_a, stats, 0)
    _, a0, _, a1 = absorb(2 * i + 1, s_b, stats, 1)
    o = a0[:vd] / a0[vd:vd + 1] - sc_ref[0] * (a1[:vd] / a1[vd:vd + 1])
    o = o * lax.rsqrt(jnp.mean(o * o, axis=0, keepdims=True) + NORM_EPS) * wsub_ref[...] * sc_ref[1]
    o_ref[...] = o.T.astype(o_ref.dtype)


def _causal_attn(qd, kb, vb, scal, wsub, nb, seq):
    t, dw = qd.shape
    heads = dw // DIFF_V_DIM
    tile = min(ATTN_TILE, seq)
    nq = seq // tile
    kt = tile // 2
    nk = seq // kt
    vt = vb.reshape(nb, nk, kt, heads, DIFF_V_DIM).transpose(0, 3, 1, 4, 2)
    vt = jnp.concatenate([vt, jnp.ones((nb, heads, nk, ONES_ROWS, kt), vt.dtype)], axis=3)
    return pl.pallas_call(
        functools.partial(_causal_attn_kernel, tile=tile),
        grid=(nb, heads, nq),
        in_specs=[pl.BlockSpec(memory_space=pltpu.SMEM),
                  pl.BlockSpec((tile, DIFF_V_DIM), lambda b, h, i: (b * nq + i, h)),
                  pl.BlockSpec((seq, DIFF_V_DIM), lambda b, h, i: (b, h)),
                  pl.BlockSpec((1, 1, nk, DIFF_V_DIM + ONES_ROWS, kt), lambda b, h, i: (b, h, 0, 0, 0)),
                  pl.BlockSpec((DIFF_V_DIM, 1), lambda b, h, i: (0, 0))],
        out_specs=pl.BlockSpec((tile, DIFF_V_DIM), lambda b, h, i: (b * nq + i, h)),
        out_shape=jax.ShapeDtypeStruct((t, dw), BF16),
        scratch_shapes=[pltpu.VMEM((2, kt, tile), F32), pltpu.VMEM((2, kt, tile), F32)],
        compiler_params=_params(3),
        name="causal_diff_attn",
    )(scal, qd, kb, vt, wsub.reshape(DIFF_V_DIM, 1))


def _paged_attn_kernel(*refs, heads, dec_seq, n_pages):
    pt_ref, sc_ref, q_ref = refs[0], refs[1], refs[2]
    k_pages = refs[3:3 + n_pages]
    v_pages = refs[3 + n_pages:3 + 2 * n_pages]
    kn_ref, vn_ref, wsub_ref, o_ref, m_ref, l_ref, acc_ref = refs[3 + 2 * n_pages:]
    del pt_ref
    g = pl.program_id(1)
    per_head = 2 * dec_seq
    rows = heads * per_head

    @pl.when(g == 0)
    def _():
        m_ref[...] = jnp.full(m_ref.shape, -jnp.inf, F32)
        l_ref[...] = jnp.zeros(l_ref.shape, F32)
        acc_ref[...] = jnp.zeros(acc_ref.shape, F32)

    q = q_ref[...].astype(BF16)
    parts = []
    for h in range(heads):
        parts.extend(_split_components(q[:, h * DIFF_V_DIM:(h + 1) * DIFF_V_DIM]))
    q_all = jnp.concatenate(parts, axis=0)

    def accumulate(kb, vb, keep):
        s = lax.dot_general(q_all, kb, _NT, preferred_element_type=F32)
        m, l, acc = _online_update(jnp.where(keep, s, -jnp.inf), vb, m_ref[...], l_ref[...], acc_ref[...])
        m_ref[...], l_ref[...], acc_ref[...] = m, l, acc

    n_keys = n_pages * k_pages[0].shape[0]
    col = lax.broadcasted_iota(jnp.int32, (rows, n_keys), 1)
    row = lax.broadcasted_iota(jnp.int32, (rows, n_keys), 0)
    accumulate(jnp.concatenate([kp[...].astype(BF16) for kp in k_pages], axis=0),
               jnp.concatenate([vp[...].astype(BF16) for vp in v_pages], axis=0),
               col % heads == row // per_head)

    @pl.when(g == pl.num_programs(1) - 1)
    def _():
        n_new = kn_ref.shape[1]
        col = lax.broadcasted_iota(jnp.int32, (rows, n_new), 1)
        row = lax.broadcasted_iota(jnp.int32, (rows, n_new), 0)
        keep = (col % heads == row // per_head) & (col // heads <= row % dec_seq)
        accumulate(kn_ref[0].astype(BF16), vn_ref[0].astype(BF16), keep)
        o = acc_ref[...] / l_ref[...]
        for h in range(heads):
            o0 = o[h * per_head:h * per_head + dec_seq]
            o1 = o[h * per_head + dec_seq:(h + 1) * per_head]
            o_ref[:, h * DIFF_V_DIM:(h + 1) * DIFF_V_DIM] = _diff_finish(o0, o1, sc_ref, wsub_ref).astype(o_ref.dtype)


def _paged_attn(qd, k_new, v_new, cache_k2, cache_v2, page_ids, page, scal, wsub, nb, dec_seq):
    t, dw = qd.shape
    heads = dw // DIFF_V_DIM
    page_rows = page * heads
    n_pages = page_ids.shape[1]
    gp = min(PAGES_PER_STEP, n_pages)
    n_steps = n_pages // gp
    new_rows = -(-dec_seq * heads // LANES) * LANES
    pad = lambda a: jnp.pad(a.reshape(nb, dec_seq * heads, DIFF_V_DIM), ((0, 0), (0, new_rows - dec_seq * heads), (0, 0)))
    page_spec = lambda i: pl.BlockSpec((page_rows, DIFF_V_DIM), lambda b, g, pt: (pt[b, g * gp + i], 0))
    tok_spec = pl.BlockSpec((dec_seq, dw), lambda b, g, pt: (b, 0))
    new_spec = pl.BlockSpec((1, new_rows, DIFF_V_DIM), lambda b, g, pt: (b, 0, 0))
    rows = heads * 2 * dec_seq
    grid_spec = pltpu.PrefetchScalarGridSpec(
        num_scalar_prefetch=1,
        grid=(nb, n_steps),
        in_specs=([pl.BlockSpec(memory_space=pltpu.SMEM), tok_spec]
                  + [page_spec(i) for i in range(gp)] * 2
                  + [new_spec, new_spec, pl.BlockSpec((1, DIFF_V_DIM), lambda b, g, pt: (0, 0))]),
        out_specs=tok_spec,
        scratch_shapes=[pltpu.VMEM((rows, 1), F32), pltpu.VMEM((rows, 1), F32), pltpu.VMEM((rows, DIFF_V_DIM), F32)],
    )
    return pl.pallas_call(
        functools.partial(_paged_attn_kernel, heads=heads, dec_seq=dec_seq, n_pages=gp),
        grid_spec=grid_spec,
        out_shape=jax.ShapeDtypeStruct((t, dw), F32),
        compiler_params=_params(2),
        name="paged_diff_attn",
    )(page_ids, scal, qd, *([cache_k2] * gp), *([cache_v2] * gp), pad(k_new), pad(v_new),
      wsub.reshape(1, DIFF_V_DIM))


def _out_proj_kernel(x_ref, oh_ref, od_ref, wo_ref, wn_ref, wr_hi_ref, wr_lo_ref, br_ref, xn_all_ref,
                     x2_ref, xn_ref, lg_ref, *, hw):
    del xn_all_ref
    mix = (jnp.dot(oh_ref[...].astype(BF16), wo_ref[:hw, :], preferred_element_type=F32)
           + jnp.dot(od_ref[...].astype(BF16), wo_ref[hw:, :], preferred_element_type=F32))
    x2 = x_ref[...] + mix
    x2_ref[...] = x2
    xn = _rms(x2, wn_ref[...])
    n_sub = xn.shape[1] // LANES
    for s in range(n_sub):
        xn_ref[pl.ds(s, xn.shape[0], stride=n_sub), :] = xn[:, s * LANES:(s + 1) * LANES]
    xn_hi = xn.astype(BF16)
    xn_lo = (xn - xn_hi.astype(F32)).astype(BF16)
    lg_ref[...] = (jnp.dot(xn_hi, wr_hi_ref[...], preferred_element_type=F32)
                   + jnp.dot(xn_lo, wr_hi_ref[...], preferred_element_type=F32)
                   + jnp.dot(xn_hi, wr_lo_ref[...], preferred_element_type=F32)
                   + br_ref[...])


def _out_proj(x2d, row0, o_h, o_d, wo_bf, wn, w_router, b_router, tok0, xn_all):
    d = x2d.shape[1]
    t, hw = o_h.shape
    dw = o_d.shape[1]
    n_e = w_router.shape[1]
    tm = min(ROW_TILE, t)
    tile0 = row0 // tm
    xn_tile0 = tok0 // tm
    wr = jnp.pad(w_router.astype(F32), ((0, 0), (0, LANES - n_e)))
    wr_hi = wr.astype(BF16)
    wr_lo = (wr - wr_hi.astype(F32)).astype(BF16)
    br = jnp.pad(b_router.astype(F32), (0, LANES - n_e), constant_values=-jnp.inf).reshape(1, LANES)
    row = lambda i: (i, 0)
    fixed = lambda i: (0, 0)
    n_sub = d // LANES
    return pl.pallas_call(
        functools.partial(_out_proj_kernel, hw=hw),
        grid=(t // tm,),
        in_specs=[pl.BlockSpec((tm, d), lambda i: (tile0 + i, 0)), pl.BlockSpec((tm, hw), row),
                  pl.BlockSpec((tm, dw), row),
                  pl.BlockSpec(wo_bf.shape, fixed), pl.BlockSpec((1, d), fixed),
                  pl.BlockSpec((d, LANES), fixed), pl.BlockSpec((d, LANES), fixed), pl.BlockSpec((1, LANES), fixed),
                  pl.BlockSpec(memory_space=pl.ANY)],
        out_specs=[pl.BlockSpec((tm, d), row), pl.BlockSpec((tm * n_sub, LANES), lambda i: (xn_tile0 + i, 0)),
                   pl.BlockSpec((tm, LANES), row)],
        out_shape=[jax.ShapeDtypeStruct((t, d), F32), jax.ShapeDtypeStruct(xn_all.shape, F32),
                   jax.ShapeDtypeStruct((t, LANES), F32)],
        input_output_aliases={8: 1},
        compiler_params=_params(1),
        name="out_proj_router",
    )(x2d, o_h, o_d, wo_bf, wn.reshape(1, d), wr_hi, wr_lo, br, xn_all)


def _router_kernel(lg_ref, gate_ref, idx_ref, rank_ref, cnt_ref, base_ref):
    i = pl.program_id(0)

    @pl.when(i == 0)
    def _():
        base_ref[...] = jnp.zeros(base_ref.shape, F32)

    cur = lg_ref[...]
    tm = cur.shape[0]
    lane = lax.broadcasted_iota(jnp.int32, cur.shape, 1)
    vals, idxs, hits = [], [], []
    for _ in range(TOP_K):
        mx = jnp.max(cur, axis=-1, keepdims=True)
        ix = jnp.min(jnp.where(cur == mx, lane, LANES), axis=-1, keepdims=True)
        hit = lane == ix
        cur = jnp.where(hit, -jnp.inf, cur)
        vals.append(mx)
        idxs.append(ix)
        hits.append(hit)
    es = [jnp.exp(v - vals[0]) for v in vals]
    denom = functools.reduce(lambda a, b: a + b, es)

    chosen = functools.reduce(lambda a, b: a | b, hits)
    chosen_f = jnp.where(chosen, 1.0, 0.0)
    below = (lax.broadcasted_iota(jnp.int32, (tm, tm), 1) < lax.broadcasted_iota(jnp.int32, (tm, tm), 0))
    earlier = base_ref[...] + jnp.dot(jnp.where(below, 1.0, 0.0).astype(BF16), chosen_f.astype(BF16),
                                      preferred_element_type=F32)
    lane_k = lax.broadcasted_iota(jnp.int32, (tm, TOP_K), 1)
    gate = jnp.zeros((tm, TOP_K), F32)
    idx = jnp.zeros((tm, TOP_K), jnp.int32)
    rank = jnp.zeros((tm, TOP_K), jnp.int32)
    for k in range(TOP_K):
        rk = jnp.sum(jnp.where(hits[k], earlier, 0.0), axis=-1, keepdims=True).astype(jnp.int32)
        gate = jnp.where(lane_k == k, es[k] / denom, gate)
        idx = jnp.where(lane_k == k, idxs[k], idx)
        rank = jnp.where(lane_k == k, rk, rank)
    gate_ref[...] = gate
    idx_ref[...] = idx
    rank_ref[...] = rank
    total = base_ref[...] + jnp.sum(chosen_f, axis=0, keepdims=True)
    base_ref[...] = total
    cnt_ref[...] = total.astype(jnp.int32)


def _router(logits):
    t = logits.shape[0]
    tm = math.gcd(ROW_TILE, t)
    row = lambda i: (i, 0)
    return pl.pallas_call(
        _router_kernel,
        grid=(t // tm,),
        in_specs=[pl.BlockSpec((tm, LANES), row)],
        out_specs=[pl.BlockSpec((tm, TOP_K), row)] * 3 + [pl.BlockSpec((1, LANES), lambda i: (0, 0))],
        out_shape=[jax.ShapeDtypeStruct((t, TOP_K), F32), jax.ShapeDtypeStruct((t, TOP_K), jnp.int32),
                   jax.ShapeDtypeStruct((t, TOP_K), jnp.int32), jax.ShapeDtypeStruct((1, LANES), jnp.int32)],
        scratch_shapes=[pltpu.VMEM((1, LANES), F32)],
        compiler_params=_params(1),
        name="router_topk",
    )(logits)


def _moe_ffn_kernel(be_ref, src0_ref, src_ref, dstp_ref, dstl_ref, x_hbm, wg_ref, bg_ref, wu_ref, bu_ref,
                    wd_ref, bd_ref, y_hbm, wg_bf, wu_bf, wd_bf, xbuf, xs, ybuf, sem_g, sem_s, *, tm, spare0):
    i = pl.program_id(0)
    last = pl.num_programs(0) - 1
    cur = lax.rem(i, 2)
    prev = 1 - cur
    n_sub = xbuf.shape[1] // tm

    def token_copy(idx_ref, r, slot, to_vmem):
        hbm_rows = pl.ds(pl.multiple_of(idx_ref[0, 0, r], n_sub), n_sub)
        vmem_rows = pl.ds(r * n_sub, n_sub)
        if to_vmem:
            return pltpu.make_async_copy(x_hbm.at[hbm_rows], xbuf.at[slot, vmem_rows], sem_g.at[slot])
        return pltpu.make_async_copy(ybuf.at[slot, vmem_rows], y_hbm.at[hbm_rows], sem_s.at[slot])

    def gather_rows(idx_ref, slot):
        for r in range(tm):
            token_copy(idx_ref, r, slot, True).start()

    def scatter_rows(idx_ref, slot):
        for r in range(tm):
            token_copy(idx_ref, r, slot, False).start(priority=1)

    def wait_gather(slot):
        pltpu.make_async_copy(x_hbm.at[pl.ds(0, tm * n_sub)], xbuf.at[slot], sem_g.at[slot]).wait()

    def wait_scatter(slot):
        pltpu.make_async_copy(ybuf.at[slot], y_hbm.at[pl.ds(0, tm * n_sub)], sem_s.at[slot]).wait()

    @pl.when(i == 0)
    def _():
        gather_rows(src0_ref, 0)
        ybuf[...] = jnp.zeros(ybuf.shape, F32)
        pltpu.make_async_copy(ybuf.at[0], y_hbm.at[pl.ds(spare0, tm * n_sub)], sem_s.at[0]).start()

    @pl.when((i == 0) | (be_ref[i] != be_ref[jnp.maximum(i - 1, 0)]))
    def _():
        wg_bf[...] = wg_ref[0].astype(BF16)
        wu_bf[...] = wu_ref[0].astype(BF16)
        wd_bf[...] = wd_ref[0].astype(BF16)

    wait_gather(cur)
    for s in range(n_sub):
        xs[:, s * LANES:(s + 1) * LANES] = xbuf[cur, pl.ds(s, tm, stride=n_sub), :].astype(BF16)
    gather_rows(src_ref, prev)
    scatter_rows(dstp_ref, prev)
    x = xs[...]
    g = jnp.dot(x, wg_bf[...], preferred_element_type=F32) + bg_ref[0]
    u = jnp.dot(x, wu_bf[...], preferred_element_type=F32) + bu_ref[0]
    g = jnp.minimum(g, SWIGLU_LIMIT)
    u = jnp.clip(u, -SWIGLU_LIMIT, SWIGLU_LIMIT)
    h = ((u + 1.0) * (g * jax.nn.sigmoid(SWIGLU_ALPHA * g))).astype(BF16)
    wait_scatter(cur)
    cw = 2 * LANES
    for c in range(n_sub * LANES // cw):
        yc = jnp.dot(h, wd_bf[:, c * cw:(c + 1) * cw], preferred_element_type=F32) + bd_ref[0, :, c * cw:(c + 1) * cw]
        for j in range(cw // LANES):
            s = c * (cw // LANES) + j
            ybuf[cur, pl.ds(s, tm, stride=n_sub), :] = yc[:, j * LANES:(j + 1) * LANES]

    @pl.when(i == last)
    def _():
        scatter_rows(dstl_ref, cur)
        wait_scatter(cur)
        wait_scatter(prev)
        wait_gather(prev)


def _moe_ffn(xn_rows, plan, n_out_rows, wg, bg, wu, bu, wd, bd):
    d = wg.shape[1]
    n_sub = d // LANES
    n_e, _, d_ff = wg.shape
    tm = MOE_TILE
    block_e, src_blocks, dst_blocks = plan["block_e"], plan["src_blocks"], plan["dst_blocks"]
    n_blocks = block_e.shape[0]
    w_spec = lambda k, n: pl.BlockSpec((1, k, n), lambda i, be: (be[i], 0, 0))
    idx_spec = lambda f: pl.BlockSpec((1, 1, tm), lambda i, be: (f(i), 0, 0), memory_space=pltpu.SMEM)
    grid_spec = pltpu.PrefetchScalarGridSpec(
        num_scalar_prefetch=1,
        grid=(n_blocks,),
        in_specs=[idx_spec(lambda i: 0), idx_spec(lambda i: jnp.minimum(i + 1, n_blocks - 1)),
                  idx_spec(lambda i: i), idx_spec(lambda i: n_blocks),
                  pl.BlockSpec(memory_space=pl.ANY),
                  w_spec(d, d_ff), w_spec(1, d_ff), w_spec(d, d_ff), w_spec(1, d_ff), w_spec(d_ff, d), w_spec(1, d)],
        out_specs=pl.BlockSpec(memory_space=pl.ANY),
        scratch_shapes=[pltpu.VMEM((d, d_ff), BF16), pltpu.VMEM((d, d_ff), BF16), pltpu.VMEM((d_ff, d), BF16),
                        pltpu.VMEM((2, tm * n_sub, LANES), F32), pltpu.VMEM((tm, d), BF16),
                        pltpu.VMEM((2, tm * n_sub, LANES), F32),
                        pltpu.SemaphoreType.DMA((2,)), pltpu.SemaphoreType.DMA((2,))],
    )
    return pl.pallas_call(
        functools.partial(_moe_ffn_kernel, tm=tm, spare0=plan["spare0"]),
        grid_spec=grid_spec,
        out_shape=jax.ShapeDtypeStruct((n_out_rows * n_sub, LANES), F32),
        compiler_params=_params(1),
        name="moe_ffn",
    )(block_e, src_blocks, src_blocks, dst_blocks, dst_blocks, xn_rows,
      wg, bg.reshape(n_e, 1, d_ff), wu, bu.reshape(n_e, 1, d_ff), wd, bd.reshape(n_e, 1, d))


def _plan_rows(idx, rank, counts, tm, n_sub):
    t = idx.shape[0]
    n_e = counts.shape[0]
    n_assign = t * TOP_K
    padded = (counts + tm - 1) // tm * tm
    pad_end = jnp.cumsum(padded)
    start_pad = pad_end - padded
    start_sorted = jnp.cumsum(counts) - counts
    dest = start_pad[idx] + rank
    n_rows = (n_assign + n_e * (tm - 1) + tm - 1) // tm * tm
    n_blocks = n_rows // tm
    block_start = jnp.arange(n_blocks, dtype=jnp.int32) * tm
    block_e = jnp.minimum(jnp.sum((pad_end[None, :] <= block_start[:, None]).astype(jnp.int32), axis=1), n_e - 1)
    assign = jnp.arange(n_assign, dtype=jnp.int32)
    _, sorted_assign = lax.sort((dest.reshape(-1), assign), num_keys=1)
    r = jnp.arange(n_rows, dtype=jnp.int32)
    row_e = jnp.repeat(block_e, tm)
    within = r - start_pad[row_e]
    valid = within < counts[row_e]
    a = sorted_assign[jnp.clip(start_sorted[row_e] + within, 0, n_assign - 1)]
    spare0 = n_assign
    row_src = jnp.where(valid, a // TOP_K, 0)
    row_dst = jnp.where(valid, (a % TOP_K) * t + a // TOP_K, spare0 + (2 + (r // tm) % 2) * tm + r % tm)
    before_first = spare0 + tm + jnp.arange(tm, dtype=jnp.int32)
    dst_blocks = jnp.concatenate([before_first, row_dst]).reshape(n_blocks + 1, 1, tm)
    plan = dict(block_e=block_e.astype(jnp.int32), src_blocks=row_src.reshape(n_blocks, 1, tm) * n_sub,
                dst_blocks=dst_blocks * n_sub, spare0=spare0 * n_sub)
    return plan, n_assign + 4 * tm


def _combine_kernel(x2_ref, *refs):
    y_refs, (gate_ref, wn_ref, o_ref) = refs[:TOP_K], refs[TOP_K:]
    gate = gate_ref[...]
    y = x2_ref[...]
    tm, d = y.shape
    n_sub = d // LANES
    for k in range(TOP_K):
        rows = jnp.concatenate([y_refs[k][pl.ds(s, tm, stride=n_sub), :] for s in range(n_sub)], axis=1)
        y = y + rows * gate[:, k:k + 1]
    o_ref[...] = _rms(y, wn_ref[...])


def _combine(x2, y_rows, gate, wn, t_all, tok0):
    t, d = x2.shape
    tm = min(COMBINE_TILE, t)
    n_sub = d // LANES
    row = lambda i: (i, 0)
    y_spec = lambda k: pl.BlockSpec((tm * n_sub, LANES), lambda i: ((k * t_all + tok0) // tm + i, 0))
    return pl.pallas_call(
        _combine_kernel,
        grid=(t // tm,),
        in_specs=([pl.BlockSpec((tm, d), row)] + [y_spec(k) for k in range(TOP_K)]
                  + [pl.BlockSpec((tm, TOP_K), row), pl.BlockSpec((1, d), lambda i: (0, 0))]),
        out_specs=pl.BlockSpec((tm, d), row),
        out_shape=jax.ShapeDtypeStruct((t, d), F32),
        compiler_params=_params(1),
        name="moe_combine",
    )(x2, *([y_rows] * TOP_K), gate, wn.reshape(1, d))


def _mix(x2d, row0, nb, seq, pos, s0, attend, p, tok0, xn_all):
    t = nb * seq
    hw, dw = p["hw"], p["dw"]
    act_dtype = BF16 if seq % 16 == 0 else F32
    tm = min(ROW_TILE, t)
    tabs = _rope_tables(pos)
    if seq % tm == 0:
        tab_tiles = seq // tm
    else:
        tabs = tuple(jnp.tile(a, (nb, 1)) for a in tabs)
        tab_tiles = t // tm
    qh, f, vi, g, qd, k_rows, kb, v_rows, vb = _in_proj(
        x2d, row0, t, p["w_norm_mix"], p["w_in"], p["lb"], tabs, tab_tiles, act_dtype, hw, dw)
    o_h, s_fin = _hgrn(qh, f, vi, g, p["w_hgrn_norm"], s0, nb, seq, act_dtype)
    o_d = attend(qd, kb, vb, k_rows, v_rows, nb, seq)
    x2, xn_all, logits = _out_proj(x2d, row0, o_h, o_d, p["w_out"], p["w_norm_ffn"], p["w_router"], p["b_router"],
                                   tok0, xn_all)
    return dict(x2=x2, xn_all=xn_all, logits=logits, k_rows=k_rows, v_rows=v_rows, s_fin=s_fin, tok0=tok0)


def kernel(x_prompt, x_sample, cache_k, cache_v, state_hgrn, page_table, w_norm_mix, w_in, hgrn_lb_logits, w_hgrn_norm, diff_lambda_q1, diff_lambda_k1, diff_lambda_q2, diff_lambda_k2, w_subln, w_out, w_norm_ffn, w_router, b_router, w_gate, b_gate, w_up, b_up, w_down, b_down, w_norm_final):
    depth = w_in.shape[0]
    assert depth == 1, "single-layer trunk only"
    l = 0
    batch, seq, d = x_prompt.shape
    dec_batch, dec_seq, _ = x_sample.shape
    n_pool, page = cache_k.shape[1], cache_k.shape[2]
    past_len = page_table.shape[1] * page
    hw = w_hgrn_norm.shape[1]
    dw = (w_in.shape[2] - 4 * hw) // 3
    heads = dw // DIFF_V_DIM
    lower_bounds = jnp.cumsum(jax.nn.softmax(hgrn_lb_logits.astype(F32), axis=0), axis=0)
    lam_init = 0.8 - 0.6 * math.exp(-0.3 * l)
    lam = (jnp.exp(jnp.sum(diff_lambda_q1[l].astype(F32) * diff_lambda_k1[l].astype(F32)))
           - jnp.exp(jnp.sum(diff_lambda_q2[l].astype(F32) * diff_lambda_k2[l].astype(F32))) + lam_init)
    scal = jnp.stack([lam, jnp.asarray(1.0 - lam_init, F32)]).astype(F32)
    p = dict(hw=hw, dw=dw, lb=lower_bounds[l], w_norm_mix=w_norm_mix[l], w_in=w_in[l].astype(BF16),
             w_hgrn_norm=w_hgrn_norm[l], w_out=w_out[l].astype(BF16), w_norm_ffn=w_norm_ffn[l],
             w_router=w_router[l], b_router=b_router[l],
             w_gate=w_gate[l], b_gate=b_gate[l], w_up=w_up[l], b_up=b_up[l],
             w_down=w_down[l], b_down=b_down[l], w_norm_final=w_norm_final)
    wsub = w_subln[l]

    def attend_prompt(qd, kb, vb, k_rows, v_rows, nb, sq):
        return _causal_attn(qd, kb, vb, scal, wsub, nb, sq)

    cache_k2 = cache_k.reshape(-1, cache_k.shape[-1])
    cache_v2 = cache_v.reshape(-1, cache_v.shape[-1])
    page_ids = page_table.astype(jnp.int32) + l * n_pool

    def attend_sample(qd, kb, vb, k_rows, v_rows, nb, sq):
        return _paged_attn(qd, k_rows, v_rows, cache_k2, cache_v2, page_ids, page, scal, wsub, nb, sq)

    pos_prompt = jnp.arange(seq, dtype=jnp.int32)
    pos_sample = past_len + jnp.arange(dec_seq, dtype=jnp.int32)
    xp2d = x_prompt.reshape(batch * seq, d)
    xs2d = x_sample.reshape(dec_batch * dec_seq, d)

    t_p, t_s = batch * seq, dec_batch * dec_seq
    t_all = t_p + t_s
    xn_all = jnp.zeros((t_all * (d // LANES), LANES), F32)
    prm = _mix(xp2d, 0, batch, seq, pos_prompt, None, attend_prompt, p, 0, xn_all)
    smp = _mix(xs2d, 0, dec_batch, dec_seq, pos_sample, state_hgrn[l], attend_sample, p, t_p, prm["xn_all"])
    n_e = w_router.shape[2]
    gate, idx, rank, counts = _router(jnp.concatenate([prm["logits"], smp["logits"]], axis=0))
    plan, n_out_rows = _plan_rows(idx, rank, counts[0, :n_e], MOE_TILE, d // LANES)
    y_rows = _moe_ffn(smp["xn_all"], plan, n_out_rows, p["w_gate"], p["b_gate"], p["w_up"], p["b_up"],
                      p["w_down"], p["b_down"])
    y_p, y_s = (_combine(st["x2"], y_rows, lax.dynamic_slice_in_dim(gate, st["tok0"], st["x2"].shape[0]),
                         p["w_norm_final"], t_all, st["tok0"]) for st in (prm, smp))
    cache_rows = lambda a, nb, sq: a.reshape(1, nb, sq, heads, DIFF_V_DIM)
    return (y_p.reshape(batch, seq, d), y_s.reshape(dec_batch, dec_seq, d),
            cache_rows(prm["k_rows"], batch, seq), cache_rows(prm["v_rows"], batch, seq), prm["s_fin"][None],
            cache_rows(smp["k_rows"], dec_batch, dec_seq), cache_rows(smp["v_rows"], dec_batch, dec_seq),
            smp["s_fin"][None])
```

```python
import functools
import math

import jax
import jax.numpy as jnp
from jax import lax
from jax.experimental import pallas as pl
from jax.experimental.pallas import tpu as pltpu

F32 = jnp.float32
BF16 = jnp.bfloat16

LANES = 128
HGRN_HEAD_DIM = 128
HGRN_CHUNK = 64
DIFF_HEAD_DIM = 64
DIFF_V_DIM = 2 * DIFF_HEAD_DIM
ROT_DIM = DIFF_HEAD_DIM // 4
ROPE_THETA = 500000.0
TOP_K = 4
SWIGLU_LIMIT = 7.0
SWIGLU_ALPHA = 1.702
NORM_EPS = 1e-6
VMEM_LIMIT = 56 * 1024 * 1024

ROW_TILE = 512
ATTN_TILE = 512
HGRN_TILE = 512
MOE_TILE = 256
PAGES_PER_STEP = 8
COMBINE_TILE = 256
ROUTER_TILE_MAX = 1024
ONES_ROWS = 16
QUERY_SCALE = DIFF_HEAD_DIM ** -0.5 * math.log2(math.e)

_NT = (((1,), (1,)), ((), ()))


def _params(n_axes):
    return pltpu.CompilerParams(dimension_semantics=("arbitrary",) * n_axes, vmem_limit_bytes=VMEM_LIMIT)


def _rms(x, w):
    return x * lax.rsqrt(jnp.mean(x * x, axis=-1, keepdims=True) + NORM_EPS) * w


def _in_proj_kernel(x_ref, wn_ref, w_ref, lb_ref, rc_ref, rp_ref, rm_ref,
                    qh_ref, f_ref, vi_ref, g_ref, qd_ref, kr_ref, kb_ref, vr_ref, vb_ref, *, hw, dw):
    h = _rms(x_ref[...], wn_ref[...]).astype(BF16)

    def proj(c0, width):
        return jnp.dot(h, w_ref[:, c0:c0 + width], preferred_element_type=F32)

    hq = proj(0, hw)
    qh_ref[...] = hq * jax.nn.sigmoid(hq)
    lb = lb_ref[...]
    f_ref[...] = lb + (1.0 - lb) * jax.nn.sigmoid(proj(hw, hw))
    vi_ref[...] = proj(2 * hw, hw)
    hg = proj(3 * hw, hw)
    g_ref[...] = hg * jax.nn.sigmoid(hg)

    rc, rp, rm = rc_ref[...], rp_ref[...], rm_ref[...]
    half = ROT_DIM // 2

    def rope(t):
        return t * rc + pltpu.roll(t, half, 1) * rp + pltpu.roll(t, LANES - half, 1) * rm

    dq = proj(4 * hw, dw)
    dk = proj(4 * hw + dw, dw)
    dv = proj(4 * hw + 2 * dw, dw)
    scale = QUERY_SCALE
    heads = dw // LANES
    tm = dq.shape[0]
    for c in range(heads):
        sl = slice(c * LANES, (c + 1) * LANES)
        qd_ref[:, sl] = (rope(dq[:, sl]) * scale).astype(qd_ref.dtype)
        kr = rope(dk[:, sl])
        kr_ref[pl.ds(c, tm, stride=heads), :] = kr
        vr_ref[pl.ds(c, tm, stride=heads), :] = dv[:, sl]
        kb_ref[:, sl] = kr.astype(kb_ref.dtype)
    vb_ref[...] = dv.astype(vb_ref.dtype)


def _in_proj(x2d, row0, t, wn, w_bf, lb, tabs, tab_tiles, act_dtype, hw, dw):
    d = x2d.shape[1]
    tm = min(ROW_TILE, t)
    heads = dw // LANES
    tile0 = row0 // tm
    row = lambda i: (i, 0)
    fixed = lambda i: (0, 0)
    tab = lambda i: (i % tab_tiles, 0)
    act = jax.ShapeDtypeStruct((t, dw), act_dtype)
    cache_rows = jax.ShapeDtypeStruct((t * heads, LANES), F32)
    out_shapes = [jax.ShapeDtypeStruct((t, hw), F32)] * 4 + [act, cache_rows, act, cache_rows, act]
    cache_spec = pl.BlockSpec((tm * heads, LANES), row)
    return pl.pallas_call(
        functools.partial(_in_proj_kernel, hw=hw, dw=dw),
        grid=(t // tm,),
        in_specs=[pl.BlockSpec((tm, d), lambda i: (tile0 + i, 0)),
                  pl.BlockSpec((1, d), fixed),
                  pl.BlockSpec(w_bf.shape, fixed, pipeline_mode=pl.Buffered(1)),
                  pl.BlockSpec((1, hw), fixed),
                  pl.BlockSpec((tm, LANES), tab), pl.BlockSpec((tm, LANES), tab), pl.BlockSpec((tm, LANES), tab)],
        out_specs=([pl.BlockSpec((tm, hw), row)] * 4
                   + [pl.BlockSpec((tm, dw), row), cache_spec, pl.BlockSpec((tm, dw), row), cache_spec,
                      pl.BlockSpec((tm, dw), row)]),
        out_shape=out_shapes,
        compiler_params=_params(1),
        name="in_proj",
    )(x2d, wn.reshape(1, d), w_bf, lb.reshape(1, hw), *tabs)


def _rope_tables(pos):
    half = ROT_DIM // 2
    inv = ROPE_THETA ** (-jnp.arange(0, ROT_DIM, 2, dtype=F32) / ROT_DIM)
    ang = pos.astype(F32)[:, None] * inv[None, :]
    cos, sin = jnp.cos(ang), jnp.sin(ang)
    n = pos.shape[0]
    rest = DIFF_HEAD_DIM - ROT_DIM
    zeros_h = jnp.zeros((n, half), F32)
    rc = jnp.concatenate([cos, cos, jnp.ones((n, rest), F32)], axis=1)
    rp = jnp.concatenate([zeros_h, sin, jnp.zeros((n, rest), F32)], axis=1)
    rm = jnp.concatenate([-sin, zeros_h, jnp.zeros((n, rest), F32)], axis=1)
    reps = LANES // DIFF_HEAD_DIM
    return tuple(jnp.tile(a, (1, reps)) for a in (rc, rp, rm))


def _hgrn_head(q, f, v, st):
    c_len = HGRN_CHUNK
    kdim = HGRN_HEAD_DIM
    rows = q.shape[0]
    if rows < c_len:
        pad = c_len - rows
        q = jnp.concatenate([q, jnp.zeros((pad, kdim), F32)], axis=0)
        v = jnp.concatenate([v, jnp.zeros((pad, kdim), F32)], axis=0)
        f = jnp.concatenate([f, jnp.ones((pad, kdim), F32)], axis=0)
    tl = q.shape[0]
    n_c = tl // c_len

    k = 1.0 - f
    b = jnp.log(f)
    r = lax.broadcasted_iota(jnp.int32, (tl, kdim), 0) % c_len
    s = 1
    while s < c_len:
        b = b + jnp.where(r >= s, pltpu.roll(b, s, 0), 0.0)
        s *= 2
    b3 = b.reshape(n_c, c_len, kdim)
    q3, k3, v3 = (a.reshape(n_c, c_len, kdim) for a in (q, k, v))
    bl = b3[:, c_len - 1:c_len, :]
    qe = (q3 * jnp.exp(b3)).astype(BF16)
    kd = (k3 * jnp.exp(-b3)).astype(BF16)
    kl = (k3 * jnp.exp(bl - b3)).astype(BF16)
    vb = v3.astype(BF16)
    dec = jnp.exp(bl)

    a = jnp.einsum('nck,ndk->ncd', qe, kd, preferred_element_type=F32)
    causal = (lax.broadcasted_iota(jnp.int32, (c_len, c_len), 1)
              <= lax.broadcasted_iota(jnp.int32, (c_len, c_len), 0))
    a = jnp.where(causal[None], a, 0.0).astype(BF16)
    o_intra = jnp.einsum('ncd,ndv->ncv', a, vb, preferred_element_type=F32)
    ds_t = jnp.einsum('ncv,nck->nvk', vb, kl, preferred_element_type=F32)

    outs = []
    for c in range(n_c):
        o_inter = lax.dot_general(qe[c], st.astype(BF16), _NT, preferred_element_type=F32)
        outs.append(o_intra[c] + o_inter)
        st = st * dec[c] + ds_t[c]
    o = outs[0] if n_c == 1 else jnp.concatenate(outs, axis=0)
    return o[:rows], st


def _hgrn_kernel(*refs, has_state):
    if has_state:
        q_ref, f_ref, v_ref, g_ref, wn_ref, s0_ref, o_ref, sfin_ref, st_ref = refs
    else:
        q_ref, f_ref, v_ref, g_ref, wn_ref, o_ref, sfin_ref, st_ref = refs
    j = pl.program_id(1)
    hd = HGRN_HEAD_DIM
    for h in range(st_ref.shape[0]):
        cols = slice(h * hd, (h + 1) * hd)

        @pl.when(j == 0)
        def _():
            st_ref[h] = s0_ref[0, h].T if has_state else jnp.zeros((hd, hd), F32)

        o, st = _hgrn_head(q_ref[:, cols], f_ref[:, cols], v_ref[:, cols], st_ref[h])
        st_ref[h] = st
        o_ref[:, cols] = (_rms(o, wn_ref[:, cols]) * g_ref[:, cols]).astype(o_ref.dtype)

        @pl.when(j == pl.num_programs(1) - 1)
        def _():
            sfin_ref[0, h] = st.T


def _hgrn(qh, f, vi, g, wn, s0, nb, seq, out_dtype):
    t, hw = qh.shape
    heads = hw // HGRN_HEAD_DIM
    tl = min(HGRN_TILE, seq)
    nl = seq // tl
    blk = pl.BlockSpec((tl, hw), lambda b, j: (b * nl + j, 0))
    state_blk = pl.BlockSpec((1, heads, HGRN_HEAD_DIM, HGRN_HEAD_DIM), lambda b, j: (b, 0, 0, 0))
    in_specs = [blk, blk, blk, blk, pl.BlockSpec((1, hw), lambda b, j: (0, 0))]
    args = [qh, f, vi, g, wn.reshape(1, hw)]
    if s0 is not None:
        in_specs.append(state_blk)
        args.append(s0)
    return pl.pallas_call(
        functools.partial(_hgrn_kernel, has_state=s0 is not None),
        grid=(nb, nl),
        in_specs=in_specs,
        out_specs=[blk, state_blk],
        out_shape=[jax.ShapeDtypeStruct((t, hw), out_dtype),
                   jax.ShapeDtypeStruct((nb, heads, HGRN_HEAD_DIM, HGRN_HEAD_DIM), F32)],
        scratch_shapes=[pltpu.VMEM((heads, HGRN_HEAD_DIM, HGRN_HEAD_DIM), F32)],
        compiler_params=_params(2),
        name="hgrn",
    )(*args)


def _split_components(q):
    lane = lax.broadcasted_iota(jnp.int32, q.shape, 1)
    zero = jnp.zeros_like(q)
    return jnp.where(lane < DIFF_HEAD_DIM, q, zero), jnp.where(lane >= DIFF_HEAD_DIM, q, zero)


def _online_update(s, vb, m, l, acc):
    m_new = jnp.maximum(m, jnp.max(s, axis=-1, keepdims=True))
    alpha = jnp.exp2(m - m_new)
    p = jnp.exp2(s - m_new)
    l = alpha * l + jnp.sum(p, axis=-1, keepdims=True)
    acc = alpha * acc + jnp.dot(p.astype(BF16), vb, preferred_element_type=F32)
    return m_new, l, acc


def _diff_finish(o0, o1, sc_ref, wsub_ref):
    o = o0 - sc_ref[0] * o1
    return _rms(o, wsub_ref[...]) * sc_ref[1]


def _causal_attn_kernel(sc_ref, q_ref, k_ref, vt_ref, wsub_ref, o_ref, s_a, s_b, *, tile):
    i = pl.program_id(2)
    q_parts = _split_components(q_ref[...])
    vd = DIFF_V_DIM
    kt = tile // 2

    def scores_to(step, buf):
        kb = k_ref[pl.ds(pl.multiple_of(step * kt, kt), kt), :]
        for c in range(2):
            buf[c] = lax.dot_general(kb, q_parts[c], _NT, preferred_element_type=F32)

    def absorb(step, buf, stats, diagonal_half):
        vt = vt_ref[0, 0, step]
        out = []
        for c in range(2):
            m, acc = stats[2 * c], stats[2 * c + 1]
            sc = buf[c]
            if diagonal_half is not None:
                keep = (lax.broadcasted_iota(jnp.int32, (kt, tile), 0) + diagonal_half * kt
                        <= lax.broadcasted_iota(jnp.int32, (kt, tile), 1))
                sc = jnp.where(keep, sc, -jnp.inf)
            m_new = jnp.maximum(m, jnp.max(sc, axis=0, keepdims=True))
            p = jnp.exp2(sc - m_new).astype(BF16)
            acc = jnp.exp2(m - m_new) * acc + jnp.dot(vt, p, preferred_element_type=F32)
            out += [m_new, acc]
        return tuple(out)

    def body(j, stats):
        scores_to(2 * j + 1, s_b)
        stats = absorb(2 * j, s_a, stats, None)
        scores_to(2 * j + 2, s_a)
        return absorb(2 * j + 1, s_b, stats, None)

    neg = jnp.full((1, tile), -jnp.inf, F32)
    zero_acc = jnp.zeros((vd + ONES_ROWS, tile), F32)
    scores_to(0, s_a)
    stats = lax.fori_loop(0, i, body, (neg, zero_acc, neg, zero_acc))
    scores_to(2 * i + 1, s_b)
    stats = absorb(2 * i, s_a, stats, 0)
    _, a0, _, a1 = absorb(2 * i + 1, s_b, stats, 1)
    o = a0[:vd] / a0[vd:vd + 1] - sc_ref[0] * (a1[:vd] / a1[vd:vd + 1])
    o = o * lax.rsqrt(jnp.mean(o * o, axis=0, keepdims=True) + NORM_EPS) * wsub_ref[...] * sc_ref[1]
    o_ref[...] = o.T.astype(o_ref.dtype)


def _causal_attn(qd, kb, vb, scal, wsub, nb, seq):
    t, dw = qd.shape
    heads = dw // DIFF_V_DIM
    tile = min(ATTN_TILE, seq)
    nq = seq // tile
    kt = tile // 2
    nk = seq // kt
    vt = vb.reshape(nb, nk, kt, heads, DIFF_V_DIM).transpose(0, 3, 1, 4, 2)
    vt = jnp.concatenate([vt, jnp.ones((nb, heads, nk, ONES_ROWS, kt), vt.dtype)], axis=3)
    return pl.pallas_call(
        functools.partial(_causal_attn_kernel, tile=tile),
        grid=(nb, heads, nq),
        in_specs=[pl.BlockSpec(memory_space=pltpu.SMEM),
                  pl.BlockSpec((tile, DIFF_V_DIM), lambda b, h, i: (b * nq + i, h)),
                  pl.BlockSpec((seq, DIFF_V_DIM), lambda b, h, i: (b, h)),
                  pl.BlockSpec((1, 1, nk, DIFF_V_DIM + ONES_ROWS, kt), lambda b, h, i: (b, h, 0, 0, 0)),
                  pl.BlockSpec((DIFF_V_DIM, 1), lambda b, h, i: (0, 0))],
        out_specs=pl.BlockSpec((tile, DIFF_V_DIM), lambda b, h, i: (b * nq + i, h)),
        out_shape=jax.ShapeDtypeStruct((t, dw), BF16),
        scratch_shapes=[pltpu.VMEM((2, kt, tile), F32), pltpu.VMEM((2, kt, tile), F32)],
        compiler_params=_params(3),
        name="causal_diff_attn",
    )(scal, qd, kb, vt, wsub.reshape(DIFF_V_DIM, 1))


def _paged_attn_kernel(*refs, heads, dec_seq, n_pages):
    pt_ref, sc_ref, q_ref = refs[0], refs[1], refs[2]
    k_pages = refs[3:3 + n_pages]
    v_pages = refs[3 + n_pages:3 + 2 * n_pages]
    kn_ref, vn_ref, wsub_ref, o_ref, m_ref, l_ref, acc_ref = refs[3 + 2 * n_pages:]
    del pt_ref
    g = pl.program_id(1)
    per_head = 2 * dec_seq
    rows = heads * per_head

    @pl.when(g == 0)
    def _():
        m_ref[...] = jnp.full(m_ref.shape, -jnp.inf, F32)
        l_ref[...] = jnp.zeros(l_ref.shape, F32)
        acc_ref[...] = jnp.zeros(acc_ref.shape, F32)

    q = q_ref[...].astype(BF16)
    parts = []
    for h in range(heads):
        parts.extend(_split_components(q[:, h * DIFF_V_DIM:(h + 1) * DIFF_V_DIM]))
    q_all = jnp.concatenate(parts, axis=0)

    def accumulate(kb, vb, keep):
        s = lax.dot_general(q_all, kb, _NT, preferred_element_type=F32)
        m, l, acc = _online_update(jnp.where(keep, s, -jnp.inf), vb, m_ref[...], l_ref[...], acc_ref[...])
        m_ref[...], l_ref[...], acc_ref[...] = m, l, acc

    n_keys = n_pages * k_pages[0].shape[0]
    col = lax.broadcasted_iota(jnp.int32, (rows, n_keys), 1)
    row = lax.broadcasted_iota(jnp.int32, (rows, n_keys), 0)
    accumulate(jnp.concatenate([kp[...].astype(BF16) for kp in k_pages], axis=0),
               jnp.concatenate([vp[...].astype(BF16) for vp in v_pages], axis=0),
               col % heads == row // per_head)

    @pl.when(g == pl.num_programs(1) - 1)
    def _():
        n_new = kn_ref.shape[1]
        col = lax.broadcasted_iota(jnp.int32, (rows, n_new), 1)
        row = lax.broadcasted_iota(jnp.int32, (rows, n_new), 0)
        keep = (col % heads == row // per_head) & (col // heads <= row % dec_seq)
        accumulate(kn_ref[0].astype(BF16), vn_ref[0].astype(BF16), keep)
        o = acc_ref[...] / l_ref[...]
        for h in range(heads):
            o0 = o[h * per_head:h * per_head + dec_seq]
            o1 = o[h * per_head + dec_seq:(h + 1) * per_head]
            o_ref[:, h * DIFF_V_DIM:(h + 1) * DIFF_V_DIM] = _diff_finish(o0, o1, sc_ref, wsub_ref).astype(o_ref.dtype)


def _paged_attn(qd, k_new, v_new, cache_k2, cache_v2, page_ids, page, scal, wsub, nb, dec_seq):
    t, dw = qd.shape
    heads = dw // DIFF_V_DIM
    page_rows = page * heads
    n_pages = page_ids.shape[1]
    gp = min(PAGES_PER_STEP, n_pages)
    n_steps = n_pages // gp
    new_rows = -(-dec_seq * heads // LANES) * LANES
    pad = lambda a: jnp.pad(a.reshape(nb, dec_seq * heads, DIFF_V_DIM), ((0, 0), (0, new_rows - dec_seq * heads), (0, 0)))
    page_spec = lambda i: pl.BlockSpec((page_rows, DIFF_V_DIM), lambda b, g, pt: (pt[b, g * gp + i], 0))
    tok_spec = pl.BlockSpec((dec_seq, dw), lambda b, g, pt: (b, 0))
    new_spec = pl.BlockSpec((1, new_rows, DIFF_V_DIM), lambda b, g, pt: (b, 0, 0))
    rows = heads * 2 * dec_seq
    grid_spec = pltpu.PrefetchScalarGridSpec(
        num_scalar_prefetch=1,
        grid=(nb, n_steps),
        in_specs=([pl.BlockSpec(memory_space=pltpu.SMEM), tok_spec]
                  + [page_spec(i) for i in range(gp)] * 2
                  + [new_spec, new_spec, pl.BlockSpec((1, DIFF_V_DIM), lambda b, g, pt: (0, 0))]),
        out_specs=tok_spec,
        scratch_shapes=[pltpu.VMEM((rows, 1), F32), pltpu.VMEM((rows, 1), F32), pltpu.VMEM((rows, DIFF_V_DIM), F32)],
    )
    return pl.pallas_call(
        functools.partial(_paged_attn_kernel, heads=heads, dec_seq=dec_seq, n_pages=gp),
        grid_spec=grid_spec,
        out_shape=jax.ShapeDtypeStruct((t, dw), F32),
        compiler_params=_params(2),
        name="paged_diff_attn",
    )(page_ids, scal, qd, *([cache_k2] * gp), *([cache_v2] * gp), pad(k_new), pad(v_new),
      wsub.reshape(1, DIFF_V_DIM))


def _out_proj_kernel(x_ref, oh_ref, od_ref, wo_ref, wn_ref, wr_hi_ref, wr_lo_ref, br_ref, xn_all_ref,
                     x2_ref, xn_ref, lg_ref, *, hw):
    del xn_all_ref
    mix = (jnp.dot(oh_ref[...].astype(BF16), wo_ref[:hw, :], preferred_element_type=F32)
           + jnp.dot(od_ref[...].astype(BF16), wo_ref[hw:, :], preferred_element_type=F32))
    x2 = x_ref[...] + mix
    x2_ref[...] = x2
    xn = _rms(x2, wn_ref[...])
    n_sub = xn.shape[1] // LANES
    for s in range(n_sub):
        xn_ref[pl.ds(s, xn.shape[0], stride=n_sub), :] = xn[:, s * LANES:(s + 1) * LANES]
    xn_hi = xn.astype(BF16)
    xn_lo = (xn - xn_hi.astype(F32)).astype(BF16)
    lg_ref[...] = (jnp.dot(xn_hi, wr_hi_ref[...], preferred_element_type=F32)
                   + jnp.dot(xn_lo, wr_hi_ref[...], preferred_element_type=F32)
                   + jnp.dot(xn_hi, wr_lo_ref[...], preferred_element_type=F32)
                   + br_ref[...])


def _out_proj(x2d, row0, o_h, o_d, wo_bf, wn, w_router, b_router, tok0, xn_all):
    d = x2d.shape[1]
    t, hw = o_h.shape
    dw = o_d.shape[1]
    n_e = w_router.shape[1]
    tm = min(ROW_TILE, t)
    tile0 = row0 // tm
    xn_tile0 = tok0 // tm
    wr = jnp.pad(w_router.astype(F32), ((0, 0), (0, LANES - n_e)))
    wr_hi = wr.astype(BF16)
    wr_lo = (wr - wr_hi.astype(F32)).astype(BF16)
    br = jnp.pad(b_router.astype(F32), (0, LANES - n_e), constant_values=-jnp.inf).reshape(1, LANES)
    row = lambda i: (i, 0)
    fixed = lambda i: (0, 0)
    n_sub = d // LANES
    return pl.pallas_call(
        functools.partial(_out_proj_kernel, hw=hw),
        grid=(t // tm,),
        in_specs=[pl.BlockSpec((tm, d), lambda i: (tile0 + i, 0)), pl.BlockSpec((tm, hw), row),
                  pl.BlockSpec((tm, dw), row),
                  pl.BlockSpec(wo_bf.shape, fixed), pl.BlockSpec((1, d), fixed),
                  pl.BlockSpec((d, LANES), fixed), pl.BlockSpec((d, LANES), fixed), pl.BlockSpec((1, LANES), fixed),
                  pl.BlockSpec(memory_space=pl.ANY)],
        out_specs=[pl.BlockSpec((tm, d), row), pl.BlockSpec((tm * n_sub, LANES), lambda i: (xn_tile0 + i, 0)),
                   pl.BlockSpec((tm, LANES), row)],
        out_shape=[jax.ShapeDtypeStruct((t, d), F32), jax.ShapeDtypeStruct(xn_all.shape, F32),
                   jax.ShapeDtypeStruct((t, LANES), F32)],
        input_output_aliases={8: 1},
        compiler_params=_params(1),
        name="out_proj_router",
    )(x2d, o_h, o_d, wo_bf, wn.reshape(1, d), wr_hi, wr_lo, br, xn_all)


def _router_kernel(lg_ref, gate_ref, idx_ref, rank_ref, cnt_ref, base_ref):
    i = pl.program_id(0)

    @pl.when(i == 0)
    def _():
        base_ref[...] = jnp.zeros(base_ref.shape, F32)

    cur = lg_ref[...]
    tm = cur.shape[0]
    lane = lax.broadcasted_iota(jnp.int32, cur.shape, 1)
    vals, idxs, hits = [], [], []
    for _ in range(TOP_K):
        mx = jnp.max(cur, axis=-1, keepdims=True)
        ix = jnp.min(jnp.where(cur == mx, lane, LANES), axis=-1, keepdims=True)
        hit = lane == ix
        cur = jnp.where(hit, -jnp.inf, cur)
        vals.append(mx)
        idxs.append(ix)
        hits.append(hit)
    es = [jnp.exp(v - vals[0]) for v in vals]
    denom = functools.reduce(lambda a, b: a + b, es)

    chosen = functools.reduce(lambda a, b: a | b, hits)
    chosen_f = jnp.where(chosen, 1.0, 0.0)
    below = (lax.broadcasted_iota(jnp.int32, (tm, tm), 1) < lax.broadcasted_iota(jnp.int32, (tm, tm), 0))
    earlier = base_ref[...] + jnp.dot(jnp.where(below, 1.0, 0.0).astype(BF16), chosen_f.astype(BF16),
                                      preferred_element_type=F32)
    lane_k = lax.broadcasted_iota(jnp.int32, (tm, TOP_K), 1)
    gate = jnp.zeros((tm, TOP_K), F32)
    idx = jnp.zeros((tm, TOP_K), jnp.int32)
    rank = jnp.zeros((tm, TOP_K), jnp.int32)
    for k in range(TOP_K):
        rk = jnp.sum(jnp.where(hits[k], earlier, 0.0), axis=-1, keepdims=True).astype(jnp.int32)
        gate = jnp.where(lane_k == k, es[k] / denom, gate)
        idx = jnp.where(lane_k == k, idxs[k], idx)
        rank = jnp.where(lane_k == k, rk, rank)
    gate_ref[...] = gate
    idx_ref[...] = idx
    rank_ref[...] = rank
    total = base_ref[...] + jnp.sum(chosen_f, axis=0, keepdims=True)
    base_ref[...] = total
    cnt_ref[...] = total.astype(jnp.int32)


def _router(logits):
    t = logits.shape[0]
    tm = max(c for c in range(8, ROUTER_TILE_MAX + 1, 8) if t % c == 0)
    row = lambda i: (i, 0)
    return pl.pallas_call(
        _router_kernel,
        grid=(t // tm,),
        in_specs=[pl.BlockSpec((tm, LANES), row)],
        out_specs=[pl.BlockSpec((tm, TOP_K), row)] * 3 + [pl.BlockSpec((1, LANES), lambda i: (0, 0))],
        out_shape=[jax.ShapeDtypeStruct((t, TOP_K), F32), jax.ShapeDtypeStruct((t, TOP_K), jnp.int32),
                   jax.ShapeDtypeStruct((t, TOP_K), jnp.int32), jax.ShapeDtypeStruct((1, LANES), jnp.int32)],
        scratch_shapes=[pltpu.VMEM((1, LANES), F32)],
        compiler_params=_params(1),
        name="router_topk",
    )(logits)


def _moe_ffn_kernel(be_ref, src0_ref, src1_ref, src_ref, dstp_ref, dstl_ref, x_hbm, wg_ref, bg_ref, wu_ref, bu_ref,
                    wd_ref, bd_ref, y_hbm, wg_bf, wu_bf, wd_bf, xbuf, xs, ybuf, sem_g, sem_s, *, tm, spare0):
    i = pl.program_id(0)
    last = pl.num_programs(0) - 1
    cur = lax.rem(i, 2)
    prev = 1 - cur
    xcur = lax.rem(i, 3)
    xahead = lax.rem(i + 2, 3)
    n_sub = xbuf.shape[1] // tm

    def token_copy(idx_ref, r, slot, to_vmem):
        hbm_rows = pl.ds(pl.multiple_of(idx_ref[0, 0, r], n_sub), n_sub)
        vmem_rows = pl.ds(r * n_sub, n_sub)
        if to_vmem:
            return pltpu.make_async_copy(x_hbm.at[hbm_rows], xbuf.at[slot, vmem_rows], sem_g.at[slot])
        return pltpu.make_async_copy(ybuf.at[slot, vmem_rows], y_hbm.at[hbm_rows], sem_s.at[slot])

    def gather_rows(idx_ref, slot):
        for r in range(tm):
            token_copy(idx_ref, r, slot, True).start(priority=r % 2)

    def scatter_rows(idx_ref, slot):
        for r in range(tm):
            token_copy(idx_ref, r, slot, False).start(priority=1)

    def wait_gather(slot):
        pltpu.make_async_copy(x_hbm.at[pl.ds(0, tm * n_sub)], xbuf.at[slot], sem_g.at[slot]).wait()

    def wait_scatter(slot):
        pltpu.make_async_copy(ybuf.at[slot], y_hbm.at[pl.ds(0, tm * n_sub)], sem_s.at[slot]).wait()

    @pl.when(i == 0)
    def _():
        gather_rows(src0_ref, 0)
        gather_rows(src1_ref, 1)
        ybuf[...] = jnp.zeros(ybuf.shape, F32)
        pltpu.make_async_copy(ybuf.at[0], y_hbm.at[pl.ds(spare0, tm * n_sub)], sem_s.at[0]).start()

    @pl.when((i == 0) | (be_ref[i] != be_ref[jnp.maximum(i - 1, 0)]))
    def _():
        wg_bf[...] = wg_ref[0].astype(BF16)
        wu_bf[...] = wu_ref[0].astype(BF16)
        wd_bf[...] = wd_ref[0].astype(BF16)

    wait_gather(xcur)
    for s in range(n_sub):
        xs[:, s * LANES:(s + 1) * LANES] = xbuf[xcur, pl.ds(s, tm, stride=n_sub), :].astype(BF16)
    gather_rows(src_ref, xahead)
    scatter_rows(dstp_ref, prev)
    x = xs[...]
    cw = 2 * LANES
    h_parts = []
    for c in range(wg_bf.shape[1] // cw):
        cols = slice(c * cw, (c + 1) * cw)
        g = jnp.dot(x, wg_bf[:, cols], preferred_element_type=F32) + bg_ref[0, :, cols]
        u = jnp.dot(x, wu_bf[:, cols], preferred_element_type=F32) + bu_ref[0, :, cols]
        g = jnp.minimum(g, SWIGLU_LIMIT)
        u = jnp.clip(u, -SWIGLU_LIMIT, SWIGLU_LIMIT)
        h_parts.append(((u + 1.0) * (g * jax.nn.sigmoid(SWIGLU_ALPHA * g))).astype(BF16))
    h = jnp.concatenate(h_parts, axis=1)
    wait_scatter(cur)
    for c in range(n_sub * LANES // cw):
        yc = jnp.dot(h, wd_bf[:, c * cw:(c + 1) * cw], preferred_element_type=F32) + bd_ref[0, :, c * cw:(c + 1) * cw]
        for j in range(cw // LANES):
            s = c * (cw // LANES) + j
            ybuf[cur, pl.ds(s, tm, stride=n_sub), :] = yc[:, j * LANES:(j + 1) * LANES]

    @pl.when(i == last)
    def _():
        scatter_rows(dstl_ref, cur)
        wait_scatter(cur)
        wait_scatter(prev)
        wait_gather(lax.rem(i + 1, 3))
        wait_gather(xahead)


def _moe_ffn(xn_rows, plan, n_out_rows, wg, bg, wu, bu, wd, bd):
    d = wg.shape[1]
    n_sub = d // LANES
    n_e, _, d_ff = wg.shape
    tm = MOE_TILE
    block_e, src_blocks, dst_blocks = plan["block_e"], plan["src_blocks"], plan["dst_blocks"]
    n_blocks = block_e.shape[0]
    w_spec = lambda k, n: pl.BlockSpec((1, k, n), lambda i, be: (be[i], 0, 0))
    idx_spec = lambda f: pl.BlockSpec((1, 1, tm), lambda i, be: (f(i), 0, 0), memory_space=pltpu.SMEM)
    grid_spec = pltpu.PrefetchScalarGridSpec(
        num_scalar_prefetch=1,
        grid=(n_blocks,),
        in_specs=[idx_spec(lambda i: 0), idx_spec(lambda i: min(1, n_blocks - 1)),
                  idx_spec(lambda i: jnp.minimum(i + 2, n_blocks - 1)),
                  idx_spec(lambda i: i), idx_spec(lambda i: n_blocks),
                  pl.BlockSpec(memory_space=pl.ANY),
                  w_spec(d, d_ff), w_spec(1, d_ff), w_spec(d, d_ff), w_spec(1, d_ff), w_spec(d_ff, d), w_spec(1, d)],
        out_specs=pl.BlockSpec(memory_space=pl.ANY),
        scratch_shapes=[pltpu.VMEM((d, d_ff), BF16), pltpu.VMEM((d, d_ff), BF16), pltpu.VMEM((d_ff, d), BF16),
                        pltpu.VMEM((3, tm * n_sub, LANES), F32), pltpu.VMEM((tm, d), BF16),
                        pltpu.VMEM((2, tm * n_sub, LANES), F32),
                        pltpu.SemaphoreType.DMA((3,)), pltpu.SemaphoreType.DMA((2,))],
    )
    return pl.pallas_call(
        functools.partial(_moe_ffn_kernel, tm=tm, spare0=plan["spare0"]),
        grid_spec=grid_spec,
        out_shape=jax.ShapeDtypeStruct((n_out_rows * n_sub, LANES), F32),
        compiler_params=_params(1),
        name="moe_ffn",
    )(block_e, src_blocks, src_blocks, src_blocks, dst_blocks, dst_blocks, xn_rows,
      wg, bg.reshape(n_e, 1, d_ff), wu, bu.reshape(n_e, 1, d_ff), wd, bd.reshape(n_e, 1, d))


def _plan_rows(idx, rank, counts, tm, n_sub):
    t = idx.shape[0]
    n_e = counts.shape[0]
    n_assign = t * TOP_K
    padded = (counts + tm - 1) // tm * tm
    pad_end = jnp.cumsum(padded)
    start_pad = pad_end - padded
    start_sorted = jnp.cumsum(counts) - counts
    dest = start_pad[idx] + rank
    n_rows = (n_assign + n_e * (tm - 1) + tm - 1) // tm * tm
    n_blocks = n_rows // tm
    block_start = jnp.arange(n_blocks, dtype=jnp.int32) * tm
    block_e = jnp.minimum(jnp.sum((pad_end[None, :] <= block_start[:, None]).astype(jnp.int32), axis=1), n_e - 1)
    assign = jnp.arange(n_assign, dtype=jnp.int32)
    _, sorted_assign = lax.sort((dest.reshape(-1), assign), num_keys=1)
    r = jnp.arange(n_rows, dtype=jnp.int32)
    row_e = jnp.repeat(block_e, tm)
    within = r - start_pad[row_e]
    valid = within < counts[row_e]
    a = sorted_assign[jnp.clip(start_sorted[row_e] + within, 0, n_assign - 1)]
    spare0 = n_assign
    row_src = jnp.where(valid, a // TOP_K, 0)
    row_dst = jnp.where(valid, (a % TOP_K) * t + a // TOP_K, spare0 + (2 + (r // tm) % 2) * tm + r % tm)
    before_first = spare0 + tm + jnp.arange(tm, dtype=jnp.int32)
    dst_blocks = jnp.concatenate([before_first, row_dst]).reshape(n_blocks + 1, 1, tm)
    plan = dict(block_e=block_e.astype(jnp.int32), src_blocks=row_src.reshape(n_blocks, 1, tm) * n_sub,
                dst_blocks=dst_blocks * n_sub, spare0=spare0 * n_sub)
    return plan, n_assign + 4 * tm


def _combine_kernel(x2_ref, *refs):
    y_refs, (gate_ref, wn_ref, o_ref) = refs[:TOP_K], refs[TOP_K:]
    gate = gate_ref[...]
    y = x2_ref[...]
    tm, d = y.shape
    n_sub = d // LANES
    for k in range(TOP_K):
        rows = jnp.concatenate([y_refs[k][pl.ds(s, tm, stride=n_sub), :] for s in range(n_sub)], axis=1)
        y = y + rows * gate[:, k:k + 1]
    o_ref[...] = _rms(y, wn_ref[...])


def _combine(x2, y_rows, gate, wn, t_all, tok0):
    t, d = x2.shape
    tm = min(COMBINE_TILE, t)
    n_sub = d // LANES
    row = lambda i: (i, 0)
    y_spec = lambda k: pl.BlockSpec((tm * n_sub, LANES), lambda i: ((k * t_all + tok0) // tm + i, 0))
    return pl.pallas_call(
        _combine_kernel,
        grid=(t // tm,),
        in_specs=([pl.BlockSpec((tm, d), row)] + [y_spec(k) for k in range(TOP_K)]
                  + [pl.BlockSpec((tm, TOP_K), row), pl.BlockSpec((1, d), lambda i: (0, 0))]),
        out_specs=pl.BlockSpec((tm, d), row),
        out_shape=jax.ShapeDtypeStruct((t, d), F32),
        compiler_params=_params(1),
        name="moe_combine",
    )(x2, *([y_rows] * TOP_K), gate, wn.reshape(1, d))


def _mix(x2d, row0, nb, seq, pos, s0, attend, p, tok0, xn_all):
    t = nb * seq
    hw, dw = p["hw"], p["dw"]
    act_dtype = BF16 if seq % 16 == 0 else F32
    tm = min(ROW_TILE, t)
    tabs = _rope_tables(pos)
    if seq % tm == 0:
        tab_tiles = seq // tm
    else:
        tabs = tuple(jnp.tile(a, (nb, 1)) for a in tabs)
        tab_tiles = t // tm
    qh, f, vi, g, qd, k_rows, kb, v_rows, vb = _in_proj(
        x2d, row0, t, p["w_norm_mix"], p["w_in"], p["lb"], tabs, tab_tiles, act_dtype, hw, dw)
    o_h, s_fin = _hgrn(qh, f, vi, g, p["w_hgrn_norm"], s0, nb, seq, act_dtype)
    o_d = attend(qd, kb, vb, k_rows, v_rows, nb, seq)
    x2, xn_all, logits = _out_proj(x2d, row0, o_h, o_d, p["w_out"], p["w_norm_ffn"], p["w_router"], p["b_router"],
                                   tok0, xn_all)
    return dict(x2=x2, xn_all=xn_all, logits=logits, k_rows=k_rows, v_rows=v_rows, s_fin=s_fin, tok0=tok0)


def kernel(x_prompt, x_sample, cache_k, cache_v, state_hgrn, page_table, w_norm_mix, w_in, hgrn_lb_logits, w_hgrn_norm, diff_lambda_q1, diff_lambda_k1, diff_lambda_q2, diff_lambda_k2, w_subln, w_out, w_norm_ffn, w_router, b_router, w_gate, b_gate, w_up, b_up, w_down, b_down, w_norm_final):
    depth = w_in.shape[0]
    assert depth == 1, "single-layer trunk only"
    l = 0
    batch, seq, d = x_prompt.shape
    dec_batch, dec_seq, _ = x_sample.shape
    n_pool, page = cache_k.shape[1], cache_k.shape[2]
    past_len = page_table.shape[1] * page
    hw = w_hgrn_norm.shape[1]
    dw = (w_in.shape[2] - 4 * hw) // 3
    heads = dw // DIFF_V_DIM
    lower_bounds = jnp.cumsum(jax.nn.softmax(hgrn_lb_logits.astype(F32), axis=0), axis=0)
    lam_init = 0.8 - 0.6 * math.exp(-0.3 * l)
    lam = (jnp.exp(jnp.sum(diff_lambda_q1[l].astype(F32) * diff_lambda_k1[l].astype(F32)))
           - jnp.exp(jnp.sum(diff_lambda_q2[l].astype(F32) * diff_lambda_k2[l].astype(F32))) + lam_init)
    scal = jnp.stack([lam, jnp.asarray(1.0 - lam_init, F32)]).astype(F32)
    p = dict(hw=hw, dw=dw, lb=lower_bounds[l], w_norm_mix=w_norm_mix[l], w_in=w_in[l].astype(BF16),
             w_hgrn_norm=w_hgrn_norm[l], w_out=w_out[l].astype(BF16), w_norm_ffn=w_norm_ffn[l],
             w_router=w_router[l], b_router=b_router[l],
             w_gate=w_gate[l], b_gate=b_gate[l], w_up=w_up[l], b_up=b_up[l],
             w_down=w_down[l], b_down=b_down[l], w_norm_final=w_norm_final)
    wsub = w_subln[l]

    def attend_prompt(qd, kb, vb, k_rows, v_rows, nb, sq):
        return _causal_attn(qd, kb, vb, scal, wsub, nb, sq)

    cache_k2 = cache_k.reshape(-1, cache_k.shape[-1])
    cache_v2 = cache_v.reshape(-1, cache_v.shape[-1])
    page_ids = page_table.astype(jnp.int32) + l * n_pool

    def attend_sample(qd, kb, vb, k_rows, v_rows, nb, sq):
        return _paged_attn(qd, k_rows, v_rows, cache_k2, cache_v2, page_ids, page, scal, wsub, nb, sq)

    pos_prompt = jnp.arange(seq, dtype=jnp.int32)
    pos_sample = past_len + jnp.arange(dec_seq, dtype=jnp.int32)
    xp2d = x_prompt.reshape(batch * seq, d)
    xs2d = x_sample.reshape(dec_batch * dec_seq, d)

    t_p, t_s = batch * seq, dec_batch * dec_seq
    t_all = t_p + t_s
    xn_all = jnp.zeros((t_all * (d // LANES), LANES), F32)
    prm = _mix(xp2d, 0, batch, seq, pos_prompt, None, attend_prompt, p, 0, xn_all)
    smp = _mix(xs2d, 0, dec_batch, dec_seq, pos_sample, state_hgrn[l], attend_sample, p, t_p, prm["xn_all"])
    n_e = w_router.shape[2]
    gate, idx, rank, counts = _router(jnp.concatenate([prm["logits"], smp["logits"]], axis=0))
    plan, n_out_rows = _plan_rows(idx, rank, counts[0, :n_e], MOE_TILE, d // LANES)
    y_rows = _moe_ffn(smp["xn_all"], plan, n_out_rows, p["w_gate"], p["b_gate"], p["w_up"], p["b_up"],
                      p["w_down"], p["b_down"])
    y_p, y_s = (_combine(st["x2"], y_rows, lax.dynamic_slice_in_dim(gate, st["tok0"], st["x2"].shape[0]),
                         p["w_norm_final"], t_all, st["tok0"]) for st in (prm, smp))
    cache_rows = lambda a, nb, sq: a.reshape(1, nb, sq, heads, DIFF_V_DIM)
    return (y_p.reshape(batch, seq, d), y_s.reshape(dec_batch, dec_seq, d),
            cache_rows(prm["k_rows"], batch, seq), cache_rows(prm["v_rows"], batch, seq), prm["s_fin"][None],
            cache_rows(smp["k_rows"], dec_batch, dec_seq), cache_rows(smp["v_rows"], dec_batch, dec_seq),
            smp["s_fin"][None])
```

```python
import functools
import math

import jax
import jax.numpy as jnp
from jax import lax
from jax.experimental import pallas as pl
from jax.experimental.pallas import tpu as pltpu

F32 = jnp.float32
BF16 = jnp.bfloat16

LANES = 128
HGRN_HEAD_DIM = 128
HGRN_CHUNK = 64
DIFF_HEAD_DIM = 64
DIFF_V_DIM = 2 * DIFF_HEAD_DIM
ROT_DIM = DIFF_HEAD_DIM // 4
ROPE_THETA = 500000.0
TOP_K = 4
SWIGLU_LIMIT = 7.0
SWIGLU_ALPHA = 1.702
NORM_EPS = 1e-6
VMEM_LIMIT = 56 * 1024 * 1024

ROW_TILE = 512
ATTN_TILE = 512
HGRN_TILE = 512
MOE_TILE = 256
PAGES_PER_STEP = 16
COMBINE_TILE = 256
ROUTER_TILE_MAX = 1024
ONES_ROWS = 16
QUERY_SCALE = DIFF_HEAD_DIM ** -0.5 * math.log2(math.e)

_NT = (((1,), (1,)), ((), ()))


def _params(n_axes):
    return pltpu.CompilerParams(dimension_semantics=("arbitrary",) * n_axes, vmem_limit_bytes=VMEM_LIMIT)


def _rms(x, w):
    return x * lax.rsqrt(jnp.mean(x * x, axis=-1, keepdims=True) + NORM_EPS) * w


def _in_proj_kernel(x_ref, wn_ref, w_ref, lb_ref, rc_ref, rp_ref, rm_ref,
                    qh_ref, f_ref, vi_ref, g_ref, qd_ref, kr_ref, kb_ref, vr_ref, vb_ref, *, hw, dw):
    h = _rms(x_ref[...], wn_ref[...]).astype(BF16)

    def proj(c0, width):
        return jnp.dot(h, w_ref[:, c0:c0 + width], preferred_element_type=F32)

    hq = proj(0, hw)
    qh_ref[...] = hq * jax.nn.sigmoid(hq)
    lb = lb_ref[...]
    f_ref[...] = lb + (1.0 - lb) * jax.nn.sigmoid(proj(hw, hw))
    vi_ref[...] = proj(2 * hw, hw)
    hg = proj(3 * hw, hw)
    g_ref[...] = hg * jax.nn.sigmoid(hg)

    rc, rp, rm = rc_ref[...], rp_ref[...], rm_ref[...]
    half = ROT_DIM // 2

    def rope(t):
        return t * rc + pltpu.roll(t, half, 1) * rp + pltpu.roll(t, LANES - half, 1) * rm

    dq = proj(4 * hw, dw)
    dk = proj(4 * hw + dw, dw)
    dv = proj(4 * hw + 2 * dw, dw)
    scale = QUERY_SCALE
    heads = dw // LANES
    tm = dq.shape[0]
    for c in range(heads):
        sl = slice(c * LANES, (c + 1) * LANES)
        qd_ref[:, sl] = (rope(dq[:, sl]) * scale).astype(qd_ref.dtype)
        kr = rope(dk[:, sl])
        kr_ref[pl.ds(c, tm, stride=heads), :] = kr
        vr_ref[pl.ds(c, tm, stride=heads), :] = dv[:, sl]
        kb_ref[:, sl] = kr.astype(kb_ref.dtype)
    vb_ref[...] = dv.astype(vb_ref.dtype)


def _in_proj(x2d, row0, t, wn, w_bf, lb, tabs, tab_tiles, act_dtype, hw, dw):
    d = x2d.shape[1]
    tm = min(ROW_TILE, t)
    heads = dw // LANES
    tile0 = row0 // tm
    row = lambda i: (i, 0)
    fixed = lambda i: (0, 0)
    tab = lambda i: (i % tab_tiles, 0)
    act = jax.ShapeDtypeStruct((t, dw), act_dtype)
    cache_rows = jax.ShapeDtypeStruct((t * heads, LANES), F32)
    out_shapes = [jax.ShapeDtypeStruct((t, hw), F32)] * 4 + [act, cache_rows, act, cache_rows, act]
    cache_spec = pl.BlockSpec((tm * heads, LANES), row)
    return pl.pallas_call(
        functools.partial(_in_proj_kernel, hw=hw, dw=dw),
        grid=(t // tm,),
        in_specs=[pl.BlockSpec((tm, d), lambda i: (tile0 + i, 0)),
                  pl.BlockSpec((1, d), fixed),
                  pl.BlockSpec(w_bf.shape, fixed, pipeline_mode=pl.Buffered(1)),
                  pl.BlockSpec((1, hw), fixed),
                  pl.BlockSpec((tm, LANES), tab), pl.BlockSpec((tm, LANES), tab), pl.BlockSpec((tm, LANES), tab)],
        out_specs=([pl.BlockSpec((tm, hw), row)] * 4
                   + [pl.BlockSpec((tm, dw), row), cache_spec, pl.BlockSpec((tm, dw), row), cache_spec,
                      pl.BlockSpec((tm, dw), row)]),
        out_shape=out_shapes,
        compiler_params=_params(1),
        name="in_proj",
    )(x2d, wn.reshape(1, d), w_bf, lb.reshape(1, hw), *tabs)


def _rope_tables(pos):
    half = ROT_DIM // 2
    inv = ROPE_THETA ** (-jnp.arange(0, ROT_DIM, 2, dtype=F32) / ROT_DIM)
    ang = pos.astype(F32)[:, None] * inv[None, :]
    cos, sin = jnp.cos(ang), jnp.sin(ang)
    n = pos.shape[0]
    rest = DIFF_HEAD_DIM - ROT_DIM
    zeros_h = jnp.zeros((n, half), F32)
    rc = jnp.concatenate([cos, cos, jnp.ones((n, rest), F32)], axis=1)
    rp = jnp.concatenate([zeros_h, sin, jnp.zeros((n, rest), F32)], axis=1)
    rm = jnp.concatenate([-sin, zeros_h, jnp.zeros((n, rest), F32)], axis=1)
    reps = LANES // DIFF_HEAD_DIM
    return tuple(jnp.tile(a, (1, reps)) for a in (rc, rp, rm))


def _hgrn_head(q, f, v, st):
    c_len = HGRN_CHUNK
    kdim = HGRN_HEAD_DIM
    rows = q.shape[0]
    if rows < c_len:
        pad = c_len - rows
        q = jnp.concatenate([q, jnp.zeros((pad, kdim), F32)], axis=0)
        v = jnp.concatenate([v, jnp.zeros((pad, kdim), F32)], axis=0)
        f = jnp.concatenate([f, jnp.ones((pad, kdim), F32)], axis=0)
    tl = q.shape[0]
    n_c = tl // c_len

    k = 1.0 - f
    b = jnp.log(f)
    r = lax.broadcasted_iota(jnp.int32, (tl, kdim), 0) % c_len
    s = 1
    while s < c_len:
        b = b + jnp.where(r >= s, pltpu.roll(b, s, 0), 0.0)
        s *= 2
    b3 = b.reshape(n_c, c_len, kdim)
    q3, k3, v3 = (a.reshape(n_c, c_len, kdim) for a in (q, k, v))
    bl = b3[:, c_len - 1:c_len, :]
    qe = (q3 * jnp.exp(b3)).astype(BF16)
    kd = (k3 * jnp.exp(-b3)).astype(BF16)
    kl = (k3 * jnp.exp(bl - b3)).astype(BF16)
    vb = v3.astype(BF16)
    dec = jnp.exp(bl)

    a = jnp.einsum('nck,ndk->ncd', qe, kd, preferred_element_type=F32)
    causal = (lax.broadcasted_iota(jnp.int32, (c_len, c_len), 1)
              <= lax.broadcasted_iota(jnp.int32, (c_len, c_len), 0))
    a = jnp.where(causal[None], a, 0.0).astype(BF16)
    o_intra = jnp.einsum('ncd,ndv->ncv', a, vb, preferred_element_type=F32)
    ds_t = jnp.einsum('ncv,nck->nvk', vb, kl, preferred_element_type=F32)

    outs = []
    for c in range(n_c):
        o_inter = lax.dot_general(qe[c], st.astype(BF16), _NT, preferred_element_type=F32)
        outs.append(o_intra[c] + o_inter)
        st = st * dec[c] + ds_t[c]
    o = outs[0] if n_c == 1 else jnp.concatenate(outs, axis=0)
    return o[:rows], st


def _hgrn_kernel(*refs, has_state):
    if has_state:
        q_ref, f_ref, v_ref, g_ref, wn_ref, s0_ref, o_ref, sfin_ref, st_ref = refs
    else:
        q_ref, f_ref, v_ref, g_ref, wn_ref, o_ref, sfin_ref, st_ref = refs
    j = pl.program_id(1)
    hd = HGRN_HEAD_DIM
    for h in range(st_ref.shape[0]):
        cols = slice(h * hd, (h + 1) * hd)

        @pl.when(j == 0)
        def _():
            st_ref[h] = s0_ref[0, h].T if has_state else jnp.zeros((hd, hd), F32)

        o, st = _hgrn_head(q_ref[:, cols], f_ref[:, cols], v_ref[:, cols], st_ref[h])
        st_ref[h] = st
        o_ref[:, cols] = (_rms(o, wn_ref[:, cols]) * g_ref[:, cols]).astype(o_ref.dtype)

        @pl.when(j == pl.num_programs(1) - 1)
        def _():
            sfin_ref[0, h] = st.T


def _hgrn(qh, f, vi, g, wn, s0, nb, seq, out_dtype):
    t, hw = qh.shape
    heads = hw // HGRN_HEAD_DIM
    tl = min(HGRN_TILE, seq)
    nl = seq // tl
    blk = pl.BlockSpec((tl, hw), lambda b, j: (b * nl + j, 0))
    state_blk = pl.BlockSpec((1, heads, HGRN_HEAD_DIM, HGRN_HEAD_DIM), lambda b, j: (b, 0, 0, 0))
    in_specs = [blk, blk, blk, blk, pl.BlockSpec((1, hw), lambda b, j: (0, 0))]
    args = [qh, f, vi, g, wn.reshape(1, hw)]
    if s0 is not None:
        in_specs.append(state_blk)
        args.append(s0)
    return pl.pallas_call(
        functools.partial(_hgrn_kernel, has_state=s0 is not None),
        grid=(nb, nl),
        in_specs=in_specs,
        out_specs=[blk, state_blk],
        out_shape=[jax.ShapeDtypeStruct((t, hw), out_dtype),
                   jax.ShapeDtypeStruct((nb, heads, HGRN_HEAD_DIM, HGRN_HEAD_DIM), F32)],
        scratch_shapes=[pltpu.VMEM((heads, HGRN_HEAD_DIM, HGRN_HEAD_DIM), F32)],
        compiler_params=_params(2),
        name="hgrn",
    )(*args)


def _split_components(q):
    lane = lax.broadcasted_iota(jnp.int32, q.shape, 1)
    zero = jnp.zeros_like(q)
    return jnp.where(lane < DIFF_HEAD_DIM, q, zero), jnp.where(lane >= DIFF_HEAD_DIM, q, zero)


def _online_update(s, vb, m, l, acc):
    m_new = jnp.maximum(m, jnp.max(s, axis=-1, keepdims=True))
    alpha = jnp.exp2(m - m_new)
    p = jnp.exp2(s - m_new)
    l = alpha * l + jnp.sum(p, axis=-1, keepdims=True)
    acc = alpha * acc + jnp.dot(p.astype(BF16), vb, preferred_element_type=F32)
    return m_new, l, acc


def _diff_finish(o0, o1, sc_ref, wsub_ref):
    o = o0 - sc_ref[0] * o1
    return _rms(o, wsub_ref[...]) * sc_ref[1]


def _causal_attn_kernel(sc_ref, q_ref, k_ref, vt_ref, wsub_ref, o_ref, s_a, s_b, *, tile):
    i = pl.program_id(2)
    q_parts = _split_components(q_ref[...])
    vd = DIFF_V_DIM
    kt = tile // 2

    def scores_to(step, buf):
        kb = k_ref[pl.ds(pl.multiple_of(step * kt, kt), kt), :]
        for c in range(2):
            buf[c] = lax.dot_general(kb, q_parts[c], _NT, preferred_element_type=F32)

    def absorb(step, buf, stats, diagonal_half):
        vt = vt_ref[0, 0, step]
        out = []
        for c in range(2):
            m, acc = stats[2 * c], stats[2 * c + 1]
            sc = buf[c]
            if diagonal_half is not None:
                keep = (lax.broadcasted_iota(jnp.int32, (kt, tile), 0) + diagonal_half * kt
                        <= lax.broadcasted_iota(jnp.int32, (kt, tile), 1))
                sc = jnp.where(keep, sc, -jnp.inf)
            m_new = jnp.maximum(m, jnp.max(sc, axis=0, keepdims=True))
            p = jnp.exp2(sc - m_new).astype(BF16)
            acc = jnp.exp2(m - m_new) * acc + jnp.dot(vt, p, preferred_element_type=F32)
            out += [m_new, acc]
        return tuple(out)

    def body(j, stats):
        scores_to(2 * j + 1, s_b)
        stats = absorb(2 * j, s_a, stats, None)
        scores_to(2 * j + 2, s_a)
        return absorb(2 * j + 1, s_b, stats, None)

    neg = jnp.full((1, tile), -jnp.inf, F32)
    zero_acc = jnp.zeros((vd + ONES_ROWS, tile), F32)
    scores_to(0, s_a)
    stats = lax.fori_loop(0, i, body, (neg, zero_acc, neg, zero_acc))
    scores_to(2 * i + 1, s_b)
    stats = absorb(2 * i, s_a, stats, 0)
    _, a0, _, a1 = absorb(2 * i + 1, s_b, stats, 1)
    o = a0[:vd] / a0[vd:vd + 1] - sc_ref[0] * (a1[:vd] / a1[vd:vd + 1])
    o = o * lax.rsqrt(jnp.mean(o * o, axis=0, keepdims=True) + NORM_EPS) * wsub_ref[...] * sc_ref[1]
    o_ref[...] = o.T.astype(o_ref.dtype)


def _causal_attn(qd, kb, vb, scal, wsub, nb, seq):
    t, dw = qd.shape
    heads = dw // DIFF_V_DIM
    tile = min(ATTN_TILE, seq)
    nq = seq // tile
    kt = tile // 2
    nk = seq // kt
    vt = vb.reshape(nb, nk, kt, heads, DIFF_V_DIM).transpose(0, 3, 1, 4, 2)
    vt = jnp.concatenate([vt, jnp.ones((nb, heads, nk, ONES_ROWS, kt), vt.dtype)], axis=3)
    return pl.pallas_call(
        functools.partial(_causal_attn_kernel, tile=tile),
        grid=(nb, heads, nq),
        in_specs=[pl.BlockSpec(memory_space=pltpu.SMEM),
                  pl.BlockSpec((tile, DIFF_V_DIM), lambda b, h, i: (b * nq + i, h)),
                  pl.BlockSpec((seq, DIFF_V_DIM), lambda b, h, i: (b, h)),
                  pl.BlockSpec((1, 1, nk, DIFF_V_DIM + ONES_ROWS, kt), lambda b, h, i: (b, h, 0, 0, 0)),
                  pl.BlockSpec((DIFF_V_DIM, 1), lambda b, h, i: (0, 0))],
        out_specs=pl.BlockSpec((tile, DIFF_V_DIM), lambda b, h, i: (b * nq + i, h)),
        out_shape=jax.ShapeDtypeStruct((t, dw), BF16),
        scratch_shapes=[pltpu.VMEM((2, kt, tile), F32), pltpu.VMEM((2, kt, tile), F32)],
        compiler_params=_params(3),
        name="causal_diff_attn",
    )(scal, qd, kb, vt, wsub.reshape(DIFF_V_DIM, 1))


def _paged_attn_kernel(*refs, heads, dec_seq, n_pages):
    pt_ref, sc_ref, q_ref = refs[0], refs[1], refs[2]
    k_pages = refs[3:3 + n_pages]
    v_pages = refs[3 + n_pages:3 + 2 * n_pages]
    kn_ref, vn_ref, wsub_ref, o_ref, m_ref, l_ref, acc_ref = refs[3 + 2 * n_pages:]
    del pt_ref
    g = pl.program_id(1)
    per_head = 2 * dec_seq
    rows = heads * per_head

    @pl.when(g == 0)
    def _():
        m_ref[...] = jnp.full(m_ref.shape, -jnp.inf, F32)
        l_ref[...] = jnp.zeros(l_ref.shape, F32)
        acc_ref[...] = jnp.zeros(acc_ref.shape, F32)

    q = q_ref[...].astype(BF16)
    parts = []
    for h in range(heads):
        parts.extend(_split_components(q[:, h * DIFF_V_DIM:(h + 1) * DIFF_V_DIM]))
    q_all = jnp.concatenate(parts, axis=0)

    def accumulate(kb, vb, keep):
        s = lax.dot_general(q_all, kb, _NT, preferred_element_type=F32)
        m, l, acc = _online_update(jnp.where(keep, s, -jnp.inf), vb, m_ref[...], l_ref[...], acc_ref[...])
        m_ref[...], l_ref[...], acc_ref[...] = m, l, acc

    n_keys = n_pages * k_pages[0].shape[0]
    col = lax.broadcasted_iota(jnp.int32, (rows, n_keys), 1)
    row = lax.broadcasted_iota(jnp.int32, (rows, n_keys), 0)
    accumulate(jnp.concatenate([kp[...].astype(BF16) for kp in k_pages], axis=0),
               jnp.concatenate([vp[...].astype(BF16) for vp in v_pages], axis=0),
               col % heads == row // per_head)

    @pl.when(g == pl.num_programs(1) - 1)
    def _():
        n_new = kn_ref.shape[1]
        col = lax.broadcasted_iota(jnp.int32, (rows, n_new), 1)
        row = lax.broadcasted_iota(jnp.int32, (rows, n_new), 0)
        keep = (col % heads == row // per_head) & (col // heads <= row % dec_seq)
        accumulate(kn_ref[0].astype(BF16), vn_ref[0].astype(BF16), keep)
        o = acc_ref[...] / l_ref[...]
        for h in range(heads):
            o0 = o[h * per_head:h * per_head + dec_seq]
            o1 = o[h * per_head + dec_seq:(h + 1) * per_head]
            o_ref[:, h * DIFF_V_DIM:(h + 1) * DIFF_V_DIM] = _diff_finish(o0, o1, sc_ref, wsub_ref).astype(o_ref.dtype)


def _paged_attn(qd, k_new, v_new, cache_k2, cache_v2, page_ids, page, scal, wsub, nb, dec_seq):
    t, dw = qd.shape
    heads = dw // DIFF_V_DIM
    page_rows = page * heads
    n_pages = page_ids.shape[1]
    gp = min(PAGES_PER_STEP, n_pages)
    n_steps = n_pages // gp
    new_rows = -(-dec_seq * heads // LANES) * LANES
    pad = lambda a: jnp.pad(a.reshape(nb, dec_seq * heads, DIFF_V_DIM), ((0, 0), (0, new_rows - dec_seq * heads), (0, 0)))
    page_spec = lambda i: pl.BlockSpec((page_rows, DIFF_V_DIM), lambda b, g, pt: (pt[b, g * gp + i], 0))
    tok_spec = pl.BlockSpec((dec_seq, dw), lambda b, g, pt: (b, 0))
    new_spec = pl.BlockSpec((1, new_rows, DIFF_V_DIM), lambda b, g, pt: (b, 0, 0))
    rows = heads * 2 * dec_seq
    grid_spec = pltpu.PrefetchScalarGridSpec(
        num_scalar_prefetch=1,
        grid=(nb, n_steps),
        in_specs=([pl.BlockSpec(memory_space=pltpu.SMEM), tok_spec]
                  + [page_spec(i) for i in range(gp)] * 2
                  + [new_spec, new_spec, pl.BlockSpec((1, DIFF_V_DIM), lambda b, g, pt: (0, 0))]),
        out_specs=tok_spec,
        scratch_shapes=[pltpu.VMEM((rows, 1), F32), pltpu.VMEM((rows, 1), F32), pltpu.VMEM((rows, DIFF_V_DIM), F32)],
    )
    return pl.pallas_call(
        functools.partial(_paged_attn_kernel, heads=heads, dec_seq=dec_seq, n_pages=gp),
        grid_spec=grid_spec,
        out_shape=jax.ShapeDtypeStruct((t, dw), F32),
        compiler_params=_params(2),
        name="paged_diff_attn",
    )(page_ids, scal, qd, *([cache_k2] * gp), *([cache_v2] * gp), pad(k_new), pad(v_new),
      wsub.reshape(1, DIFF_V_DIM))


def _out_proj_kernel(x_ref, oh_ref, od_ref, wo_ref, wn_ref, wr_hi_ref, wr_lo_ref, br_ref, xn_all_ref,
                     x2_ref, xn_ref, lg_ref, *, hw):
    del xn_all_ref
    mix = (jnp.dot(oh_ref[...].astype(BF16), wo_ref[:hw, :], preferred_element_type=F32)
           + jnp.dot(od_ref[...].astype(BF16), wo_ref[hw:, :], preferred_element_type=F32))
    x2 = x_ref[...] + mix
    x2_ref[...] = x2
    xn = _rms(x2, wn_ref[...])
    n_sub = xn.shape[1] // LANES
    for s in range(n_sub):
        xn_ref[pl.ds(s, xn.shape[0], stride=n_sub), :] = xn[:, s * LANES:(s + 1) * LANES]
    xn_hi = xn.astype(BF16)
    xn_lo = (xn - xn_hi.astype(F32)).astype(BF16)
    lg_ref[...] = (jnp.dot(xn_hi, wr_hi_ref[...], preferred_element_type=F32)
                   + jnp.dot(xn_lo, wr_hi_ref[...], preferred_element_type=F32)
                   + jnp.dot(xn_hi, wr_lo_ref[...], preferred_element_type=F32)
                   + br_ref[...])


def _out_proj(x2d, row0, o_h, o_d, wo_bf, wn, w_router, b_router, tok0, xn_all):
    d = x2d.shape[1]
    t, hw = o_h.shape
    dw = o_d.shape[1]
    n_e = w_router.shape[1]
    tm = min(ROW_TILE, t)
    tile0 = row0 // tm
    xn_tile0 = tok0 // tm
    wr = jnp.pad(w_router.astype(F32), ((0, 0), (0, LANES - n_e)))
    wr_hi = wr.astype(BF16)
    wr_lo = (wr - wr_hi.astype(F32)).astype(BF16)
    br = jnp.pad(b_router.astype(F32), (0, LANES - n_e), constant_values=-jnp.inf).reshape(1, LANES)
    row = lambda i: (i, 0)
    fixed = lambda i: (0, 0)
    n_sub = d // LANES
    return pl.pallas_call(
        functools.partial(_out_proj_kernel, hw=hw),
        grid=(t // tm,),
        in_specs=[pl.BlockSpec((tm, d), lambda i: (tile0 + i, 0)), pl.BlockSpec((tm, hw), row),
                  pl.BlockSpec((tm, dw), row),
                  pl.BlockSpec(wo_bf.shape, fixed), pl.BlockSpec((1, d), fixed),
                  pl.BlockSpec((d, LANES), fixed), pl.BlockSpec((d, LANES), fixed), pl.BlockSpec((1, LANES), fixed),
                  pl.BlockSpec(memory_space=pl.ANY)],
        out_specs=[pl.BlockSpec((tm, d), row), pl.BlockSpec((tm * n_sub, LANES), lambda i: (xn_tile0 + i, 0)),
                   pl.BlockSpec((tm, LANES), row)],
        out_shape=[jax.ShapeDtypeStruct((t, d), F32), jax.ShapeDtypeStruct(xn_all.shape, F32),
                   jax.ShapeDtypeStruct((t, LANES), F32)],
        input_output_aliases={8: 1},
        compiler_params=_params(1),
        name="out_proj_router",
    )(x2d, o_h, o_d, wo_bf, wn.reshape(1, d), wr_hi, wr_lo, br, xn_all)


def _router_kernel(lg_ref, gate_ref, idx_ref, cnt_ref, base_ref):
    i = pl.program_id(0)

    @pl.when(i == 0)
    def _():
        base_ref[...] = jnp.zeros(base_ref.shape, F32)

    cur = lg_ref[...]
    tm = cur.shape[0]
    lane = lax.broadcasted_iota(jnp.int32, cur.shape, 1)
    vals, idxs, hits = [], [], []
    for _ in range(TOP_K):
        mx = jnp.max(cur, axis=-1, keepdims=True)
        ix = jnp.min(jnp.where(cur == mx, lane, LANES), axis=-1, keepdims=True)
        hit = lane == ix
        cur = jnp.where(hit, -jnp.inf, cur)
        vals.append(mx)
        idxs.append(ix)
        hits.append(hit)
    es = [jnp.exp(v - vals[0]) for v in vals]
    denom = functools.reduce(lambda a, b: a + b, es)

    chosen = functools.reduce(lambda a, b: a | b, hits)
    chosen_f = jnp.where(chosen, 1.0, 0.0)
    lane_k = lax.broadcasted_iota(jnp.int32, (tm, TOP_K), 1)
    gate = jnp.zeros((tm, TOP_K), F32)
    idx = jnp.zeros((tm, TOP_K), jnp.int32)
    for k in range(TOP_K):
        gate = jnp.where(lane_k == k, es[k] / denom, gate)
        idx = jnp.where(lane_k == k, idxs[k], idx)
    gate_ref[...] = gate
    idx_ref[...] = idx
    total = base_ref[...] + jnp.sum(chosen_f, axis=0, keepdims=True)
    base_ref[...] = total
    cnt_ref[...] = total.astype(jnp.int32)


def _router(logits):
    t = logits.shape[0]
    tm = max(c for c in range(8, ROUTER_TILE_MAX + 1, 8) if t % c == 0)
    row = lambda i: (i, 0)
    return pl.pallas_call(
        _router_kernel,
        grid=(t // tm,),
        in_specs=[pl.BlockSpec((tm, LANES), row)],
        out_specs=[pl.BlockSpec((tm, TOP_K), row)] * 2 + [pl.BlockSpec((1, LANES), lambda i: (0, 0))],
        out_shape=[jax.ShapeDtypeStruct((t, TOP_K), F32), jax.ShapeDtypeStruct((t, TOP_K), jnp.int32),
                   jax.ShapeDtypeStruct((1, LANES), jnp.int32)],
        scratch_shapes=[pltpu.VMEM((1, LANES), F32)],
        compiler_params=_params(1),
        name="router_topk",
    )(logits)


def _moe_ffn_kernel(be_ref, src0_ref, src1_ref, src_ref, dstp_ref, dstl_ref, x_hbm, wg_ref, bg_ref, wu_ref, bu_ref,
                    wd_ref, bd_ref, y_hbm, wg_bf, wu_bf, wd_bf, xbuf, xs, ybuf, sem_g, sem_s, *, tm, spare0):
    i = pl.program_id(0)
    last = pl.num_programs(0) - 1
    cur = lax.rem(i, 2)
    prev = 1 - cur
    xcur = lax.rem(i, 3)
    xahead = lax.rem(i + 2, 3)
    n_sub = xbuf.shape[1] // tm

    def token_copy(idx_ref, r, slot, to_vmem):
        hbm_rows = pl.ds(pl.multiple_of(idx_ref[0, 0, r], n_sub), n_sub)
        vmem_rows = pl.ds(r * n_sub, n_sub)
        if to_vmem:
            return pltpu.make_async_copy(x_hbm.at[hbm_rows], xbuf.at[slot, vmem_rows], sem_g.at[slot])
        return pltpu.make_async_copy(ybuf.at[slot, vmem_rows], y_hbm.at[hbm_rows], sem_s.at[slot])

    def gather_rows(idx_ref, slot):
        for r in range(tm):
            token_copy(idx_ref, r, slot, True).start(priority=r % 2)

    def scatter_rows(idx_ref, slot):
        for r in range(tm):
            token_copy(idx_ref, r, slot, False).start(priority=1)

    def wait_gather(slot):
        pltpu.make_async_copy(x_hbm.at[pl.ds(0, tm * n_sub)], xbuf.at[slot], sem_g.at[slot]).wait()

    def wait_scatter(slot):
        pltpu.make_async_copy(ybuf.at[slot], y_hbm.at[pl.ds(0, tm * n_sub)], sem_s.at[slot]).wait()

    @pl.when(i == 0)
    def _():
        gather_rows(src0_ref, 0)
        gather_rows(src1_ref, 1)
        ybuf[...] = jnp.zeros(ybuf.shape, F32)
        pltpu.make_async_copy(ybuf.at[0], y_hbm.at[pl.ds(spare0, tm * n_sub)], sem_s.at[0]).start()

    @pl.when((i == 0) | (be_ref[i] != be_ref[jnp.maximum(i - 1, 0)]))
    def _():
        wg_bf[...] = wg_ref[0].astype(BF16)
        wu_bf[...] = wu_ref[0].astype(BF16)
        wd_bf[...] = wd_ref[0].astype(BF16)

    wait_gather(xcur)
    for s in range(n_sub):
        xs[:, s * LANES:(s + 1) * LANES] = xbuf[xcur, pl.ds(s, tm, stride=n_sub), :].astype(BF16)
    gather_rows(src_ref, xahead)
    scatter_rows(dstp_ref, prev)
    x = xs[...]
    cw = 2 * LANES
    h_parts = []
    for c in range(wg_bf.shape[1] // cw):
        cols = slice(c * cw, (c + 1) * cw)
        g = jnp.dot(x, wg_bf[:, cols], preferred_element_type=F32) + bg_ref[0, :, cols]
        u = jnp.dot(x, wu_bf[:, cols], preferred_element_type=F32) + bu_ref[0, :, cols]
        g = jnp.minimum(g, SWIGLU_LIMIT)
        u = jnp.clip(u, -SWIGLU_LIMIT, SWIGLU_LIMIT)
        h_parts.append(((u + 1.0) * (g * jax.nn.sigmoid(SWIGLU_ALPHA * g))).astype(BF16))
    h = jnp.concatenate(h_parts, axis=1)
    wait_scatter(cur)
    for c in range(n_sub * LANES // cw):
        yc = jnp.dot(h, wd_bf[:, c * cw:(c + 1) * cw], preferred_element_type=F32) + bd_ref[0, :, c * cw:(c + 1) * cw]
        for j in range(cw // LANES):
            s = c * (cw // LANES) + j
            ybuf[cur, pl.ds(s, tm, stride=n_sub), :] = yc[:, j * LANES:(j + 1) * LANES]

    @pl.when(i == last)
    def _():
        scatter_rows(dstl_ref, cur)
        wait_scatter(cur)
        wait_scatter(prev)
        wait_gather(lax.rem(i + 1, 3))
        wait_gather(xahead)


def _moe_ffn(xn_rows, plan, n_out_rows, wg, bg, wu, bu, wd, bd):
    d = wg.shape[1]
    n_sub = d // LANES
    n_e, _, d_ff = wg.shape
    tm = MOE_TILE
    block_e, src_blocks, dst_blocks = plan["block_e"], plan["src_blocks"], plan["dst_blocks"]
    n_blocks = block_e.shape[0]
    w_spec = lambda k, n: pl.BlockSpec((1, k, n), lambda i, be: (be[i], 0, 0))
    idx_spec = lambda f: pl.BlockSpec((1, 1, tm), lambda i, be: (f(i), 0, 0), memory_space=pltpu.SMEM)
    grid_spec = pltpu.PrefetchScalarGridSpec(
        num_scalar_prefetch=1,
        grid=(n_blocks,),
        in_specs=[idx_spec(lambda i: 0), idx_spec(lambda i: min(1, n_blocks - 1)),
                  idx_spec(lambda i: jnp.minimum(i + 2, n_blocks - 1)),
                  idx_spec(lambda i: i), idx_spec(lambda i: n_blocks),
                  pl.BlockSpec(memory_space=pl.ANY),
                  w_spec(d, d_ff), w_spec(1, d_ff), w_spec(d, d_ff), w_spec(1, d_ff), w_spec(d_ff, d), w_spec(1, d)],
        out_specs=pl.BlockSpec(memory_space=pl.ANY),
        scratch_shapes=[pltpu.VMEM((d, d_ff), BF16), pltpu.VMEM((d, d_ff), BF16), pltpu.VMEM((d_ff, d), BF16),
                        pltpu.VMEM((3, tm * n_sub, LANES), F32), pltpu.VMEM((tm, d), BF16),
                        pltpu.VMEM((2, tm * n_sub, LANES), F32),
                        pltpu.SemaphoreType.DMA((3,)), pltpu.SemaphoreType.DMA((2,))],
    )
    return pl.pallas_call(
        functools.partial(_moe_ffn_kernel, tm=tm, spare0=plan["spare0"]),
        grid_spec=grid_spec,
        out_shape=jax.ShapeDtypeStruct((n_out_rows * n_sub, LANES), F32),
        compiler_params=_params(1),
        name="moe_ffn",
    )(block_e, src_blocks, src_blocks, src_blocks, dst_blocks, dst_blocks, xn_rows,
      wg, bg.reshape(n_e, 1, d_ff), wu, bu.reshape(n_e, 1, d_ff), wd, bd.reshape(n_e, 1, d))


def _plan_rows(idx, counts, tm, n_sub):
    t = idx.shape[0]
    n_e = counts.shape[0]
    n_assign = t * TOP_K
    padded = (counts + tm - 1) // tm * tm
    pad_end = jnp.cumsum(padded)
    start_pad = pad_end - padded
    start_sorted = jnp.cumsum(counts) - counts
    n_rows = (n_assign + n_e * (tm - 1) + tm - 1) // tm * tm
    n_blocks = n_rows // tm
    block_start = jnp.arange(n_blocks, dtype=jnp.int32) * tm
    block_e = jnp.minimum(jnp.sum((pad_end[None, :] <= block_start[:, None]).astype(jnp.int32), axis=1), n_e - 1)
    span = 1 << (n_assign - 1).bit_length()
    assert n_e * span < 2 ** 31
    keys = idx.reshape(-1).astype(jnp.int32) * span + jnp.arange(n_assign, dtype=jnp.int32)
    sorted_assign = jnp.sort(keys) & (span - 1)
    r = jnp.arange(n_rows, dtype=jnp.int32)
    row_e = jnp.repeat(block_e, tm)
    within = r - start_pad[row_e]
    valid = within < counts[row_e]
    a = sorted_assign[jnp.clip(start_sorted[row_e] + within, 0, n_assign - 1)]
    spare0 = n_assign
    row_src = jnp.where(valid, a // TOP_K, 0)
    row_dst = jnp.where(valid, (a % TOP_K) * t + a // TOP_K, spare0 + (2 + (r // tm) % 2) * tm + r % tm)
    before_first = spare0 + tm + jnp.arange(tm, dtype=jnp.int32)
    dst_blocks = jnp.concatenate([before_first, row_dst]).reshape(n_blocks + 1, 1, tm)
    plan = dict(block_e=block_e.astype(jnp.int32), src_blocks=row_src.reshape(n_blocks, 1, tm) * n_sub,
                dst_blocks=dst_blocks * n_sub, spare0=spare0 * n_sub)
    return plan, n_assign + 4 * tm


def _combine_kernel(x2_ref, *refs):
    y_refs, (gate_ref, wn_ref, o_ref) = refs[:TOP_K], refs[TOP_K:]
    gate = gate_ref[...]
    y = x2_ref[...]
    tm, d = y.shape
    n_sub = d // LANES
    for k in range(TOP_K):
        rows = jnp.concatenate([y_refs[k][pl.ds(s, tm, stride=n_sub), :] for s in range(n_sub)], axis=1)
        y = y + rows * gate[:, k:k + 1]
    o_ref[...] = _rms(y, wn_ref[...])


def _combine(x2, y_rows, gate, wn, t_all, tok0):
    t, d = x2.shape
    tm = min(COMBINE_TILE, t)
    n_sub = d // LANES
    row = lambda i: (i, 0)
    y_spec = lambda k: pl.BlockSpec((tm * n_sub, LANES), lambda i: ((k * t_all + tok0) // tm + i, 0))
    return pl.pallas_call(
        _combine_kernel,
        grid=(t // tm,),
        in_specs=([pl.BlockSpec((tm, d), row)] + [y_spec(k) for k in range(TOP_K)]
                  + [pl.BlockSpec((tm, TOP_K), row), pl.BlockSpec((1, d), lambda i: (0, 0))]),
        out_specs=pl.BlockSpec((tm, d), row),
        out_shape=jax.ShapeDtypeStruct((t, d), F32),
        compiler_params=_params(1),
        name="moe_combine",
    )(x2, *([y_rows] * TOP_K), gate, wn.reshape(1, d))


def _mix(x2d, row0, nb, seq, pos, s0, attend, p, tok0, xn_all):
    t = nb * seq
    hw, dw = p["hw"], p["dw"]
    act_dtype = BF16 if seq % 16 == 0 else F32
    tm = min(ROW_TILE, t)
    tabs = _rope_tables(pos)
    if seq % tm == 0:
        tab_tiles = seq // tm
    else:
        tabs = tuple(jnp.tile(a, (nb, 1)) for a in tabs)
        tab_tiles = t // tm
    qh, f, vi, g, qd, k_rows, kb, v_rows, vb = _in_proj(
        x2d, row0, t, p["w_norm_mix"], p["w_in"], p["lb"], tabs, tab_tiles, act_dtype, hw, dw)
    o_h, s_fin = _hgrn(qh, f, vi, g, p["w_hgrn_norm"], s0, nb, seq, act_dtype)
    o_d = attend(qd, kb, vb, k_rows, v_rows, nb, seq)
    x2, xn_all, logits = _out_proj(x2d, row0, o_h, o_d, p["w_out"], p["w_norm_ffn"], p["w_router"], p["b_router"],
                                   tok0, xn_all)
    return dict(x2=x2, xn_all=xn_all, logits=logits, k_rows=k_rows, v_rows=v_rows, s_fin=s_fin, tok0=tok0)


def kernel(x_prompt, x_sample, cache_k, cache_v, state_hgrn, page_table, w_norm_mix, w_in, hgrn_lb_logits, w_hgrn_norm, diff_lambda_q1, diff_lambda_k1, diff_lambda_q2, diff_lambda_k2, w_subln, w_out, w_norm_ffn, w_router, b_router, w_gate, b_gate, w_up, b_up, w_down, b_down, w_norm_final):
    depth = w_in.shape[0]
    assert depth == 1, "single-layer trunk only"
    l = 0
    batch, seq, d = x_prompt.shape
    dec_batch, dec_seq, _ = x_sample.shape
    n_pool, page = cache_k.shape[1], cache_k.shape[2]
    past_len = page_table.shape[1] * page
    hw = w_hgrn_norm.shape[1]
    dw = (w_in.shape[2] - 4 * hw) // 3
    heads = dw // DIFF_V_DIM
    lower_bounds = jnp.cumsum(jax.nn.softmax(hgrn_lb_logits.astype(F32), axis=0), axis=0)
    lam_init = 0.8 - 0.6 * math.exp(-0.3 * l)
    lam = (jnp.exp(jnp.sum(diff_lambda_q1[l].astype(F32) * diff_lambda_k1[l].astype(F32)))
           - jnp.exp(jnp.sum(diff_lambda_q2[l].astype(F32) * diff_lambda_k2[l].astype(F32))) + lam_init)
    scal = jnp.stack([lam, jnp.asarray(1.0 - lam_init, F32)]).astype(F32)
    p = dict(hw=hw, dw=dw, lb=lower_bounds[l], w_norm_mix=w_norm_mix[l], w_in=w_in[l].astype(BF16),
             w_hgrn_norm=w_hgrn_norm[l], w_out=w_out[l].astype(BF16), w_norm_ffn=w_norm_ffn[l],
             w_router=w_router[l], b_router=b_router[l],
             w_gate=w_gate[l], b_gate=b_gate[l], w_up=w_up[l], b_up=b_up[l],
             w_down=w_down[l], b_down=b_down[l], w_norm_final=w_norm_final)
    wsub = w_subln[l]

    def attend_prompt(qd, kb, vb, k_rows, v_rows, nb, sq):
        return _causal_attn(qd, kb, vb, scal, wsub, nb, sq)

    cache_k2 = cache_k.reshape(-1, cache_k.shape[-1])
    cache_v2 = cache_v.reshape(-1, cache_v.shape[-1])
    page_ids = page_table.astype(jnp.int32) + l * n_pool

    def attend_sample(qd, kb, vb, k_rows, v_rows, nb, sq):
        return _paged_attn(qd, k_rows, v_rows, cache_k2, cache_v2, page_ids, page, scal, wsub, nb, sq)

    pos_prompt = jnp.arange(seq, dtype=jnp.int32)
    pos_sample = past_len + jnp.arange(dec_seq, dtype=jnp.int32)
    xp2d = x_prompt.reshape(batch * seq, d)
    xs2d = x_sample.reshape(dec_batch * dec_seq, d)

    t_p, t_s = batch * seq, dec_batch * dec_seq
    t_all = t_p + t_s
    xn_all = jnp.zeros((t_all * (d // LANES), LANES), F32)
    prm = _mix(xp2d, 0, batch, seq, pos_prompt, None, attend_prompt, p, 0, xn_all)
    smp = _mix(xs2d, 0, dec_batch, dec_seq, pos_sample, state_hgrn[l], attend_sample, p, t_p, prm["xn_all"])
    n_e = w_router.shape[2]
    gate, idx, counts = _router(jnp.concatenate([prm["logits"], smp["logits"]], axis=0))
    plan, n_out_rows = _plan_rows(idx, counts[0, :n_e], MOE_TILE, d // LANES)
    y_rows = _moe_ffn(smp["xn_all"], plan, n_out_rows, p["w_gate"], p["b_gate"], p["w_up"], p["b_up"],
                      p["w_down"], p["b_down"])
    y_p, y_s = (_combine(st["x2"], y_rows, lax.dynamic_slice_in_dim(gate, st["tok0"], st["x2"].shape[0]),
                         p["w_norm_final"], t_all, st["tok0"]) for st in (prm, smp))
    cache_rows = lambda a, nb, sq: a.reshape(1, nb, sq, heads, DIFF_V_DIM)
    return (y_p.reshape(batch, seq, d), y_s.reshape(dec_batch, dec_seq, d),
            cache_rows(prm["k_rows"], batch, seq), cache_rows(prm["v_rows"], batch, seq), prm["s_fin"][None],
            cache_rows(smp["k_rows"], dec_batch, dec_seq), cache_rows(smp["v_rows"], dec_batch, dec_seq),
            smp["s_fin"][None])
```

```python
import functools
import math

import jax
import jax.numpy as jnp
from jax import lax
from jax.experimental import pallas as pl
from jax.experimental.pallas import tpu as pltpu

F32 = jnp.float32
BF16 = jnp.bfloat16

LANES = 128
HGRN_HEAD_DIM = 128
HGRN_CHUNK = 64
DIFF_HEAD_DIM = 64
DIFF_V_DIM = 2 * DIFF_HEAD_DIM
ROT_DIM = DIFF_HEAD_DIM // 4
ROPE_THETA = 500000.0
TOP_K = 4
SWIGLU_LIMIT = 7.0
SWIGLU_ALPHA = 1.702
NORM_EPS = 1e-6
VMEM_LIMIT = 56 * 1024 * 1024

ROW_TILE = 512
ATTN_TILE = 1024
HGRN_TILE = 512
MOE_TILE = 256
PAGES_PER_STEP = 16
COMBINE_TILE = 256
ROUTER_TILE_MAX = 1024
ONES_ROWS = 16
QUERY_SCALE = DIFF_HEAD_DIM ** -0.5 * math.log2(math.e)

_NT = (((1,), (1,)), ((), ()))


def _params(n_axes):
    return pltpu.CompilerParams(dimension_semantics=("arbitrary",) * n_axes, vmem_limit_bytes=VMEM_LIMIT)


def _rms(x, w):
    return x * lax.rsqrt(jnp.mean(x * x, axis=-1, keepdims=True) + NORM_EPS) * w


def _in_proj_kernel(x_ref, wn_ref, w_ref, lb_ref, rc_ref, rp_ref, rm_ref,
                    qh_ref, f_ref, vi_ref, g_ref, qd_ref, kr_ref, kb_ref, vr_ref, vb_ref, *, hw, dw):
    h = _rms(x_ref[...], wn_ref[...]).astype(BF16)

    def proj(c0, width):
        return jnp.dot(h, w_ref[:, c0:c0 + width], preferred_element_type=F32)

    hq = proj(0, hw)
    qh_ref[...] = hq * jax.nn.sigmoid(hq)
    lb = lb_ref[...]
    f_ref[...] = lb + (1.0 - lb) * jax.nn.sigmoid(proj(hw, hw))
    vi_ref[...] = proj(2 * hw, hw)
    hg = proj(3 * hw, hw)
    g_ref[...] = hg * jax.nn.sigmoid(hg)

    rc, rp, rm = rc_ref[...], rp_ref[...], rm_ref[...]
    half = ROT_DIM // 2

    def rope(t):
        return t * rc + pltpu.roll(t, half, 1) * rp + pltpu.roll(t, LANES - half, 1) * rm

    dq = proj(4 * hw, dw)
    dk = proj(4 * hw + dw, dw)
    dv = proj(4 * hw + 2 * dw, dw)
    scale = QUERY_SCALE
    heads = dw // LANES
    tm = dq.shape[0]
    for c in range(heads):
        sl = slice(c * LANES, (c + 1) * LANES)
        qd_ref[:, sl] = (rope(dq[:, sl]) * scale).astype(qd_ref.dtype)
        kr = rope(dk[:, sl])
        kr_ref[pl.ds(c, tm, stride=heads), :] = kr
        vr_ref[pl.ds(c, tm, stride=heads), :] = dv[:, sl]
        kb_ref[:, sl] = kr.astype(kb_ref.dtype)
    vb_ref[...] = dv.astype(vb_ref.dtype)


def _in_proj(x2d, row0, t, wn, w_bf, lb, tabs, tab_tiles, act_dtype, hw, dw):
    d = x2d.shape[1]
    tm = min(ROW_TILE, t)
    heads = dw // LANES
    tile0 = row0 // tm
    row = lambda i: (i, 0)
    fixed = lambda i: (0, 0)
    tab = lambda i: (i % tab_tiles, 0)
    act = jax.ShapeDtypeStruct((t, dw), act_dtype)
    cache_rows = jax.ShapeDtypeStruct((t * heads, LANES), F32)
    out_shapes = [jax.ShapeDtypeStruct((t, hw), F32)] * 4 + [act, cache_rows, act, cache_rows, act]
    cache_spec = pl.BlockSpec((tm * heads, LANES), row)
    return pl.pallas_call(
        functools.partial(_in_proj_kernel, hw=hw, dw=dw),
        grid=(t // tm,),
        in_specs=[pl.BlockSpec((tm, d), lambda i: (tile0 + i, 0)),
                  pl.BlockSpec((1, d), fixed),
                  pl.BlockSpec(w_bf.shape, fixed, pipeline_mode=pl.Buffered(1)),
                  pl.BlockSpec((1, hw), fixed),
                  pl.BlockSpec((tm, LANES), tab), pl.BlockSpec((tm, LANES), tab), pl.BlockSpec((tm, LANES), tab)],
        out_specs=([pl.BlockSpec((tm, hw), row)] * 4
                   + [pl.BlockSpec((tm, dw), row), cache_spec, pl.BlockSpec((tm, dw), row), cache_spec,
                      pl.BlockSpec((tm, dw), row)]),
        out_shape=out_shapes,
        compiler_params=_params(1),
        name="in_proj",
    )(x2d, wn.reshape(1, d), w_bf, lb.reshape(1, hw), *tabs)


def _rope_tables(pos):
    half = ROT_DIM // 2
    inv = ROPE_THETA ** (-jnp.arange(0, ROT_DIM, 2, dtype=F32) / ROT_DIM)
    ang = pos.astype(F32)[:, None] * inv[None, :]
    cos, sin = jnp.cos(ang), jnp.sin(ang)
    n = pos.shape[0]
    rest = DIFF_HEAD_DIM - ROT_DIM
    zeros_h = jnp.zeros((n, half), F32)
    rc = jnp.concatenate([cos, cos, jnp.ones((n, rest), F32)], axis=1)
    rp = jnp.concatenate([zeros_h, sin, jnp.zeros((n, rest), F32)], axis=1)
    rm = jnp.concatenate([-sin, zeros_h, jnp.zeros((n, rest), F32)], axis=1)
    reps = LANES // DIFF_HEAD_DIM
    return tuple(jnp.tile(a, (1, reps)) for a in (rc, rp, rm))


def _hgrn_head(q, f, v, st):
    c_len = HGRN_CHUNK
    kdim = HGRN_HEAD_DIM
    rows = q.shape[0]
    if rows < c_len:
        pad = c_len - rows
        q = jnp.concatenate([q, jnp.zeros((pad, kdim), F32)], axis=0)
        v = jnp.concatenate([v, jnp.zeros((pad, kdim), F32)], axis=0)
        f = jnp.concatenate([f, jnp.ones((pad, kdim), F32)], axis=0)
    tl = q.shape[0]
    n_c = tl // c_len

    k = 1.0 - f
    b = jnp.log(f)
    r = lax.broadcasted_iota(jnp.int32, (tl, kdim), 0) % c_len
    s = 1
    while s < c_len:
        b = b + jnp.where(r >= s, pltpu.roll(b, s, 0), 0.0)
        s *= 2
    b3 = b.reshape(n_c, c_len, kdim)
    q3, k3, v3 = (a.reshape(n_c, c_len, kdim) for a in (q, k, v))
    bl = b3[:, c_len - 1:c_len, :]
    qe = (q3 * jnp.exp(b3)).astype(BF16)
    kd = (k3 * jnp.exp(-b3)).astype(BF16)
    kl = (k3 * jnp.exp(bl - b3)).astype(BF16)
    vb = v3.astype(BF16)
    dec = jnp.exp(bl)

    a = jnp.einsum('nck,ndk->ncd', qe, kd, preferred_element_type=F32)
    causal = (lax.broadcasted_iota(jnp.int32, (c_len, c_len), 1)
              <= lax.broadcasted_iota(jnp.int32, (c_len, c_len), 0))
    a = jnp.where(causal[None], a, 0.0).astype(BF16)
    o_intra = jnp.einsum('ncd,ndv->ncv', a, vb, preferred_element_type=F32)
    ds_t = jnp.einsum('ncv,nck->nvk', vb, kl, preferred_element_type=F32)

    outs = []
    for c in range(n_c):
        o_inter = lax.dot_general(qe[c], st.astype(BF16), _NT, preferred_element_type=F32)
        outs.append(o_intra[c] + o_inter)
        st = st * dec[c] + ds_t[c]
    o = outs[0] if n_c == 1 else jnp.concatenate(outs, axis=0)
    return o[:rows], st


def _hgrn_kernel(*refs, has_state):
    if has_state:
        q_ref, f_ref, v_ref, g_ref, wn_ref, s0_ref, o_ref, sfin_ref, st_ref = refs
    else:
        q_ref, f_ref, v_ref, g_ref, wn_ref, o_ref, sfin_ref, st_ref = refs
    j = pl.program_id(1)
    hd = HGRN_HEAD_DIM
    for h in range(st_ref.shape[0]):
        cols = slice(h * hd, (h + 1) * hd)

        @pl.when(j == 0)
        def _():
            st_ref[h] = s0_ref[0, h].T if has_state else jnp.zeros((hd, hd), F32)

        o, st = _hgrn_head(q_ref[:, cols], f_ref[:, cols], v_ref[:, cols], st_ref[h])
        st_ref[h] = st
        o_ref[:, cols] = (_rms(o, wn_ref[:, cols]) * g_ref[:, cols]).astype(o_ref.dtype)

        @pl.when(j == pl.num_programs(1) - 1)
        def _():
            sfin_ref[0, h] = st.T


def _hgrn(qh, f, vi, g, wn, s0, nb, seq, out_dtype):
    t, hw = qh.shape
    heads = hw // HGRN_HEAD_DIM
    tl = min(HGRN_TILE, seq)
    nl = seq // tl
    blk = pl.BlockSpec((tl, hw), lambda b, j: (b * nl + j, 0))
    state_blk = pl.BlockSpec((1, heads, HGRN_HEAD_DIM, HGRN_HEAD_DIM), lambda b, j: (b, 0, 0, 0))
    in_specs = [blk, blk, blk, blk, pl.BlockSpec((1, hw), lambda b, j: (0, 0))]
    args = [qh, f, vi, g, wn.reshape(1, hw)]
    if s0 is not None:
        in_specs.append(state_blk)
        args.append(s0)
    return pl.pallas_call(
        functools.partial(_hgrn_kernel, has_state=s0 is not None),
        grid=(nb, nl),
        in_specs=in_specs,
        out_specs=[blk, state_blk],
        out_shape=[jax.ShapeDtypeStruct((t, hw), out_dtype),
                   jax.ShapeDtypeStruct((nb, heads, HGRN_HEAD_DIM, HGRN_HEAD_DIM), F32)],
        scratch_shapes=[pltpu.VMEM((heads, HGRN_HEAD_DIM, HGRN_HEAD_DIM), F32)],
        compiler_params=_params(2),
        name="hgrn",
    )(*args)


def _split_components(q):
    lane = lax.broadcasted_iota(jnp.int32, q.shape, 1)
    zero = jnp.zeros_like(q)
    return jnp.where(lane < DIFF_HEAD_DIM, q, zero), jnp.where(lane >= DIFF_HEAD_DIM, q, zero)


def _online_update(s, vb, m, l, acc):
    m_new = jnp.maximum(m, jnp.max(s, axis=-1, keepdims=True))
    alpha = jnp.exp2(m - m_new)
    p = jnp.exp2(s - m_new)
    l = alpha * l + jnp.sum(p, axis=-1, keepdims=True)
    acc = alpha * acc + jnp.dot(p.astype(BF16), vb, preferred_element_type=F32)
    return m_new, l, acc


def _diff_finish(o0, o1, sc_ref, wsub_ref):
    o = o0 - sc_ref[0] * o1
    return _rms(o, wsub_ref[...]) * sc_ref[1]


def _causal_attn_kernel(sc_ref, q_ref, k_ref, vt_ref, wsub_ref, o_ref, s_a, s_b, *, tile):
    i = pl.program_id(2)
    q_parts = _split_components(q_ref[...])
    vd = DIFF_V_DIM
    kt = tile // 2

    def scores_to(step, buf):
        kb = k_ref[pl.ds(pl.multiple_of(step * kt, kt), kt), :]
        for c in range(2):
            buf[c] = lax.dot_general(kb, q_parts[c], _NT, preferred_element_type=F32)

    def absorb(step, buf, stats, diagonal_half):
        vt = vt_ref[0, 0, step]
        out = []
        for c in range(2):
            m, acc = stats[2 * c], stats[2 * c + 1]
            sc = buf[c]
            if diagonal_half is not None:
                keep = (lax.broadcasted_iota(jnp.int32, (kt, tile), 0) + diagonal_half * kt
                        <= lax.broadcasted_iota(jnp.int32, (kt, tile), 1))
                sc = jnp.where(keep, sc, -jnp.inf)
            m_new = jnp.maximum(m, jnp.max(sc, axis=0, keepdims=True))
            p = jnp.exp2(sc - m_new).astype(BF16)
            acc = jnp.exp2(m - m_new) * acc + jnp.dot(vt, p, preferred_element_type=F32)
            out += [m_new, acc]
        return tuple(out)

    def body(j, stats):
        scores_to(2 * j + 1, s_b)
        stats = absorb(2 * j, s_a, stats, None)
        scores_to(2 * j + 2, s_a)
        return absorb(2 * j + 1, s_b, stats, None)

    neg = jnp.full((1, tile), -jnp.inf, F32)
    zero_acc = jnp.zeros((vd + ONES_ROWS, tile), F32)
    scores_to(0, s_a)
    stats = lax.fori_loop(0, i, body, (neg, zero_acc, neg, zero_acc))
    scores_to(2 * i + 1, s_b)
    stats = absorb(2 * i, s_a, stats, 0)
    _, a0, _, a1 = absorb(2 * i + 1, s_b, stats, 1)
    o = a0[:vd] / a0[vd:vd + 1] - sc_ref[0] * (a1[:vd] / a1[vd:vd + 1])
    o = o * lax.rsqrt(jnp.mean(o * o, axis=0, keepdims=True) + NORM_EPS) * wsub_ref[...] * sc_ref[1]
    o_ref[...] = o.T.astype(o_ref.dtype)


def _causal_attn(qd, kb, vb, scal, wsub, nb, seq):
    t, dw = qd.shape
    heads = dw // DIFF_V_DIM
    tile = min(ATTN_TILE, seq)
    nq = seq // tile
    kt = tile // 2
    nk = seq // kt
    vt = vb.reshape(nb, nk, kt, heads, DIFF_V_DIM).transpose(0, 3, 1, 4, 2)
    vt = jnp.concatenate([vt, jnp.ones((nb, heads, nk, ONES_ROWS, kt), vt.dtype)], axis=3)
    return pl.pallas_call(
        functools.partial(_causal_attn_kernel, tile=tile),
        grid=(nb, heads, nq),
        in_specs=[pl.BlockSpec(memory_space=pltpu.SMEM),
                  pl.BlockSpec((tile, DIFF_V_DIM), lambda b, h, i: (b * nq + i, h)),
                  pl.BlockSpec((seq, DIFF_V_DIM), lambda b, h, i: (b, h)),
                  pl.BlockSpec((1, 1, nk, DIFF_V_DIM + ONES_ROWS, kt), lambda b, h, i: (b, h, 0, 0, 0)),
                  pl.BlockSpec((DIFF_V_DIM, 1), lambda b, h, i: (0, 0))],
        out_specs=pl.BlockSpec((tile, DIFF_V_DIM), lambda b, h, i: (b * nq + i, h)),
        out_shape=jax.ShapeDtypeStruct((t, dw), BF16),
        scratch_shapes=[pltpu.VMEM((2, kt, tile), F32), pltpu.VMEM((2, kt, tile), F32)],
        compiler_params=_params(3),
        name="causal_diff_attn",
    )(scal, qd, kb, vt, wsub.reshape(DIFF_V_DIM, 1))


def _paged_attn_kernel(*refs, heads, dec_seq, n_pages):
    pt_ref, sc_ref, q_ref = refs[0], refs[1], refs[2]
    k_pages = refs[3:3 + n_pages]
    v_pages = refs[3 + n_pages:3 + 2 * n_pages]
    kn_ref, vn_ref, wsub_ref, o_ref, m_ref, l_ref, acc_ref = refs[3 + 2 * n_pages:]
    del pt_ref
    g = pl.program_id(1)
    per_head = 2 * dec_seq
    rows = heads * per_head

    @pl.when(g == 0)
    def _():
        m_ref[...] = jnp.full(m_ref.shape, -jnp.inf, F32)
        l_ref[...] = jnp.zeros(l_ref.shape, F32)
        acc_ref[...] = jnp.zeros(acc_ref.shape, F32)

    q = q_ref[...].astype(BF16)
    parts = []
    for h in range(heads):
        parts.extend(_split_components(q[:, h * DIFF_V_DIM:(h + 1) * DIFF_V_DIM]))
    q_all = jnp.concatenate(parts, axis=0)

    def accumulate(kb, vb, keep):
        s = lax.dot_general(q_all, kb, _NT, preferred_element_type=F32)
        m, l, acc = _online_update(jnp.where(keep, s, -jnp.inf), vb, m_ref[...], l_ref[...], acc_ref[...])
        m_ref[...], l_ref[...], acc_ref[...] = m, l, acc

    n_keys = n_pages * k_pages[0].shape[0]
    col = lax.broadcasted_iota(jnp.int32, (rows, n_keys), 1)
    row = lax.broadcasted_iota(jnp.int32, (rows, n_keys), 0)
    accumulate(jnp.concatenate([kp[...].astype(BF16) for kp in k_pages], axis=0),
               jnp.concatenate([vp[...].astype(BF16) for vp in v_pages], axis=0),
               col % heads == row // per_head)

    @pl.when(g == pl.num_programs(1) - 1)
    def _():
        n_new = kn_ref.shape[1]
        col = lax.broadcasted_iota(jnp.int32, (rows, n_new), 1)
        row = lax.broadcasted_iota(jnp.int32, (rows, n_new), 0)
        keep = (col % heads == row // per_head) & (col // heads <= row % dec_seq)
        accumulate(kn_ref[0].astype(BF16), vn_ref[0].astype(BF16), keep)
        o = acc_ref[...] / l_ref[...]
        for h in range(heads):
            o0 = o[h * per_head:h * per_head + dec_seq]
            o1 = o[h * per_head + dec_seq:(h + 1) * per_head]
            o_ref[:, h * DIFF_V_DIM:(h + 1) * DIFF_V_DIM] = _diff_finish(o0, o1, sc_ref, wsub_ref).astype(o_ref.dtype)


def _paged_attn(qd, k_new, v_new, cache_k2, cache_v2, page_ids, page, scal, wsub, nb, dec_seq):
    t, dw = qd.shape
    heads = dw // DIFF_V_DIM
    page_rows = page * heads
    n_pages = page_ids.shape[1]
    gp = min(PAGES_PER_STEP, n_pages)
    n_steps = n_pages // gp
    new_rows = -(-dec_seq * heads // LANES) * LANES
    pad = lambda a: jnp.pad(a.reshape(nb, dec_seq * heads, DIFF_V_DIM), ((0, 0), (0, new_rows - dec_seq * heads), (0, 0)))
    page_spec = lambda i: pl.BlockSpec((page_rows, DIFF_V_DIM), lambda b, g, pt: (pt[b, g * gp + i], 0))
    tok_spec = pl.BlockSpec((dec_seq, dw), lambda b, g, pt: (b, 0))
    new_spec = pl.BlockSpec((1, new_rows, DIFF_V_DIM), lambda b, g, pt: (b, 0, 0))
    rows = heads * 2 * dec_seq
    grid_spec = pltpu.PrefetchScalarGridSpec(
        num_scalar_prefetch=1,
        grid=(nb, n_steps),
        in_specs=([pl.BlockSpec(memory_space=pltpu.SMEM), tok_spec]
                  + [page_spec(i) for i in range(gp)] * 2
                  + [new_spec, new_spec, pl.BlockSpec((1, DIFF_V_DIM), lambda b, g, pt: (0, 0))]),
        out_specs=tok_spec,
        scratch_shapes=[pltpu.VMEM((rows, 1), F32), pltpu.VMEM((rows, 1), F32), pltpu.VMEM((rows, DIFF_V_DIM), F32)],
    )
    return pl.pallas_call(
        functools.partial(_paged_attn_kernel, heads=heads, dec_seq=dec_seq, n_pages=gp),
        grid_spec=grid_spec,
        out_shape=jax.ShapeDtypeStruct((t, dw), F32),
        compiler_params=_params(2),
        name="paged_diff_attn",
    )(page_ids, scal, qd, *([cache_k2] * gp), *([cache_v2] * gp), pad(k_new), pad(v_new),
      wsub.reshape(1, DIFF_V_DIM))


def _out_proj_kernel(x_ref, oh_ref, od_ref, wo_ref, wn_ref, wr_hi_ref, wr_lo_ref, br_ref, xn_all_ref,
                     x2_ref, xn_ref, lg_ref, *, hw):
    del xn_all_ref
    mix = (jnp.dot(oh_ref[...].astype(BF16), wo_ref[:hw, :], preferred_element_type=F32)
           + jnp.dot(od_ref[...].astype(BF16), wo_ref[hw:, :], preferred_element_type=F32))
    x2 = x_ref[...] + mix
    x2_ref[...] = x2
    xn = _rms(x2, wn_ref[...])
    n_sub = xn.shape[1] // LANES
    for s in range(n_sub):
        xn_ref[pl.ds(s, xn.shape[0], stride=n_sub), :] = xn[:, s * LANES:(s + 1) * LANES]
    xn_hi = xn.astype(BF16)
    xn_lo = (xn - xn_hi.astype(F32)).astype(BF16)
    lg_ref[...] = (jnp.dot(xn_hi, wr_hi_ref[...], preferred_element_type=F32)
                   + jnp.dot(xn_lo, wr_hi_ref[...], preferred_element_type=F32)
                   + jnp.dot(xn_hi, wr_lo_ref[...], preferred_element_type=F32)
                   + br_ref[...])


def _out_proj(x2d, row0, o_h, o_d, wo_bf, wn, w_router, b_router, tok0, xn_all):
    d = x2d.shape[1]
    t, hw = o_h.shape
    dw = o_d.shape[1]
    n_e = w_router.shape[1]
    tm = min(ROW_TILE, t)
    tile0 = row0 // tm
    xn_tile0 = tok0 // tm
    wr = jnp.pad(w_router.astype(F32), ((0, 0), (0, LANES - n_e)))
    wr_hi = wr.astype(BF16)
    wr_lo = (wr - wr_hi.astype(F32)).astype(BF16)
    br = jnp.pad(b_router.astype(F32), (0, LANES - n_e), constant_values=-jnp.inf).reshape(1, LANES)
    row = lambda i: (i, 0)
    fixed = lambda i: (0, 0)
    n_sub = d // LANES
    return pl.pallas_call(
        functools.partial(_out_proj_kernel, hw=hw),
        grid=(t // tm,),
        in_specs=[pl.BlockSpec((tm, d), lambda i: (tile0 + i, 0)), pl.BlockSpec((tm, hw), row),
                  pl.BlockSpec((tm, dw), row),
                  pl.BlockSpec(wo_bf.shape, fixed), pl.BlockSpec((1, d), fixed),
                  pl.BlockSpec((d, LANES), fixed), pl.BlockSpec((d, LANES), fixed), pl.BlockSpec((1, LANES), fixed),
                  pl.BlockSpec(memory_space=pl.ANY)],
        out_specs=[pl.BlockSpec((tm, d), row), pl.BlockSpec((tm * n_sub, LANES), lambda i: (xn_tile0 + i, 0)),
                   pl.BlockSpec((tm, LANES), row)],
        out_shape=[jax.ShapeDtypeStruct((t, d), F32), jax.ShapeDtypeStruct(xn_all.shape, F32),
                   jax.ShapeDtypeStruct((t, LANES), F32)],
        input_output_aliases={8: 1},
        compiler_params=_params(1),
        name="out_proj_router",
    )(x2d, o_h, o_d, wo_bf, wn.reshape(1, d), wr_hi, wr_lo, br, xn_all)


def _router_kernel(lg_ref, gate_ref, idx_ref, cnt_ref, base_ref):
    i = pl.program_id(0)

    @pl.when(i == 0)
    def _():
        base_ref[...] = jnp.zeros(base_ref.shape, F32)

    cur = lg_ref[...]
    tm = cur.shape[0]
    lane = lax.broadcasted_iota(jnp.int32, cur.shape, 1)
    vals, idxs, hits = [], [], []
    for _ in range(TOP_K):
        mx = jnp.max(cur, axis=-1, keepdims=True)
        ix = jnp.min(jnp.where(cur == mx, lane, LANES), axis=-1, keepdims=True)
        hit = lane == ix
        cur = jnp.where(hit, -jnp.inf, cur)
        vals.append(mx)
        idxs.append(ix)
        hits.append(hit)
    es = [jnp.exp(v - vals[0]) for v in vals]
    denom = functools.reduce(lambda a, b: a + b, es)

    chosen = functools.reduce(lambda a, b: a | b, hits)
    chosen_f = jnp.where(chosen, 1.0, 0.0)
    lane_k = lax.broadcasted_iota(jnp.int32, (tm, TOP_K), 1)
    gate = jnp.zeros((tm, TOP_K), F32)
    idx = jnp.zeros((tm, TOP_K), jnp.int32)
    for k in range(TOP_K):
        gate = jnp.where(lane_k == k, es[k] / denom, gate)
        idx = jnp.where(lane_k == k, idxs[k], idx)
    gate_ref[...] = gate
    idx_ref[...] = idx
    total = base_ref[...] + jnp.sum(chosen_f, axis=0, keepdims=True)
    base_ref[...] = total
    cnt_ref[...] = total.astype(jnp.int32)


def _router(logits):
    t = logits.shape[0]
    tm = max(c for c in range(8, ROUTER_TILE_MAX + 1, 8) if t % c == 0)
    row = lambda i: (i, 0)
    return pl.pallas_call(
        _router_kernel,
        grid=(t // tm,),
        in_specs=[pl.BlockSpec((tm, LANES), row)],
        out_specs=[pl.BlockSpec((tm, TOP_K), row)] * 2 + [pl.BlockSpec((1, LANES), lambda i: (0, 0))],
        out_shape=[jax.ShapeDtypeStruct((t, TOP_K), F32), jax.ShapeDtypeStruct((t, TOP_K), jnp.int32),
                   jax.ShapeDtypeStruct((1, LANES), jnp.int32)],
        scratch_shapes=[pltpu.VMEM((1, LANES), F32)],
        compiler_params=_params(1),
        name="router_topk",
    )(logits)


def _moe_ffn_kernel(be_ref, src0_ref, src1_ref, src_ref, dstp_ref, dstl_ref, x_hbm, wg_ref, bg_ref, wu_ref, bu_ref,
                    wd_ref, bd_ref, y_hbm, wg_bf, wu_bf, wd_bf, xbuf, xs, ybuf, sem_g, sem_s, *, tm, spare0):
    i = pl.program_id(0)
    last = pl.num_programs(0) - 1
    cur = lax.rem(i, 2)
    prev = 1 - cur
    xcur = lax.rem(i, 3)
    xahead = lax.rem(i + 2, 3)
    n_sub = xbuf.shape[1] // tm

    def token_copy(idx_ref, r, slot, to_vmem):
        hbm_rows = pl.ds(pl.multiple_of(idx_ref[0, 0, r], n_sub), n_sub)
        vmem_rows = pl.ds(r * n_sub, n_sub)
        if to_vmem:
            return pltpu.make_async_copy(x_hbm.at[hbm_rows], xbuf.at[slot, vmem_rows], sem_g.at[slot])
        return pltpu.make_async_copy(ybuf.at[slot, vmem_rows], y_hbm.at[hbm_rows], sem_s.at[slot])

    def gather_rows(idx_ref, slot):
        for r in range(tm):
            token_copy(idx_ref, r, slot, True).start(priority=r % 2)

    def scatter_rows(idx_ref, slot):
        for r in range(tm):
            token_copy(idx_ref, r, slot, False).start(priority=1)

    def wait_gather(slot):
        pltpu.make_async_copy(x_hbm.at[pl.ds(0, tm * n_sub)], xbuf.at[slot], sem_g.at[slot]).wait()

    def wait_scatter(slot):
        pltpu.make_async_copy(ybuf.at[slot], y_hbm.at[pl.ds(0, tm * n_sub)], sem_s.at[slot]).wait()

    @pl.when(i == 0)
    def _():
        gather_rows(src0_ref, 0)
        gather_rows(src1_ref, 1)
        ybuf[...] = jnp.zeros(ybuf.shape, F32)
        pltpu.make_async_copy(ybuf.at[0], y_hbm.at[pl.ds(spare0, tm * n_sub)], sem_s.at[0]).start()

    @pl.when((i == 0) | (be_ref[i] != be_ref[jnp.maximum(i - 1, 0)]))
    def _():
        wg_bf[...] = wg_ref[0].astype(BF16)
        wu_bf[...] = wu_ref[0].astype(BF16)
        wd_bf[...] = wd_ref[0].astype(BF16)

    wait_gather(xcur)
    for s in range(n_sub):
        xs[:, s * LANES:(s + 1) * LANES] = xbuf[xcur, pl.ds(s, tm, stride=n_sub), :].astype(BF16)
    gather_rows(src_ref, xahead)
    scatter_rows(dstp_ref, prev)
    x = xs[...]
    cw = 2 * LANES
    h_parts = []
    for c in range(wg_bf.shape[1] // cw):
        cols = slice(c * cw, (c + 1) * cw)
        g = jnp.dot(x, wg_bf[:, cols], preferred_element_type=F32) + bg_ref[0, :, cols]
        u = jnp.dot(x, wu_bf[:, cols], preferred_element_type=F32) + bu_ref[0, :, cols]
        g = jnp.minimum(g, SWIGLU_LIMIT)
        u = jnp.clip(u, -SWIGLU_LIMIT, SWIGLU_LIMIT)
        h_parts.append(((u + 1.0) * (g * jax.nn.sigmoid(SWIGLU_ALPHA * g))).astype(BF16))
    h = jnp.concatenate(h_parts, axis=1)
    wait_scatter(cur)
    for c in range(n_sub * LANES // cw):
        yc = jnp.dot(h, wd_bf[:, c * cw:(c + 1) * cw], preferred_element_type=F32) + bd_ref[0, :, c * cw:(c + 1) * cw]
        for j in range(cw // LANES):
            s = c * (cw // LANES) + j
            ybuf[cur, pl.ds(s, tm, stride=n_sub), :] = yc[:, j * LANES:(j + 1) * LANES]

    @pl.when(i == last)
    def _():
        scatter_rows(dstl_ref, cur)
        wait_scatter(cur)
        wait_scatter(prev)
        wait_gather(lax.rem(i + 1, 3))
        wait_gather(xahead)


def _moe_ffn(xn_rows, plan, n_out_rows, wg, bg, wu, bu, wd, bd):
    d = wg.shape[1]
    n_sub = d // LANES
    n_e, _, d_ff = wg.shape
    tm = MOE_TILE
    block_e, src_blocks, dst_blocks = plan["block_e"], plan["src_blocks"], plan["dst_blocks"]
    n_blocks = block_e.shape[0]
    w_spec = lambda k, n: pl.BlockSpec((1, k, n), lambda i, be: (be[i], 0, 0))
    idx_spec = lambda f: pl.BlockSpec((1, 1, tm), lambda i, be: (f(i), 0, 0), memory_space=pltpu.SMEM)
    grid_spec = pltpu.PrefetchScalarGridSpec(
        num_scalar_prefetch=1,
        grid=(n_blocks,),
        in_specs=[idx_spec(lambda i: 0), idx_spec(lambda i: min(1, n_blocks - 1)),
                  idx_spec(lambda i: jnp.minimum(i + 2, n_blocks - 1)),
                  idx_spec(lambda i: i), idx_spec(lambda i: n_blocks),
                  pl.BlockSpec(memory_space=pl.ANY),
                  w_spec(d, d_ff), w_spec(1, d_ff), w_spec(d, d_ff), w_spec(1, d_ff), w_spec(d_ff, d), w_spec(1, d)],
        out_specs=pl.BlockSpec(memory_space=pl.ANY),
        scratch_shapes=[pltpu.VMEM((d, d_ff), BF16), pltpu.VMEM((d, d_ff), BF16), pltpu.VMEM((d_ff, d), BF16),
                        pltpu.VMEM((3, tm * n_sub, LANES), F32), pltpu.VMEM((tm, d), BF16),
                        pltpu.VMEM((2, tm * n_sub, LANES), F32),
                        pltpu.SemaphoreType.DMA((3,)), pltpu.SemaphoreType.DMA((2,))],
    )
    return pl.pallas_call(
        functools.partial(_moe_ffn_kernel, tm=tm, spare0=plan["spare0"]),
        grid_spec=grid_spec,
        out_shape=jax.ShapeDtypeStruct((n_out_rows * n_sub, LANES), F32),
        compiler_params=_params(1),
        name="moe_ffn",
    )(block_e, src_blocks, src_blocks, src_blocks, dst_blocks, dst_blocks, xn_rows,
      wg, bg.reshape(n_e, 1, d_ff), wu, bu.reshape(n_e, 1, d_ff), wd, bd.reshape(n_e, 1, d))


def _plan_rows(idx, counts, tm, n_sub):
    t = idx.shape[0]
    n_e = counts.shape[0]
    n_assign = t * TOP_K
    padded = (counts + tm - 1) // tm * tm
    pad_end = jnp.cumsum(padded)
    start_pad = pad_end - padded
    start_sorted = jnp.cumsum(counts) - counts
    n_rows = (n_assign + n_e * (tm - 1) + tm - 1) // tm * tm
    n_blocks = n_rows // tm
    block_start = jnp.arange(n_blocks, dtype=jnp.int32) * tm
    block_e = jnp.minimum(jnp.sum((pad_end[None, :] <= block_start[:, None]).astype(jnp.int32), axis=1), n_e - 1)
    span = 1 << (n_assign - 1).bit_length()
    assert n_e * span < 2 ** 31
    keys = idx.reshape(-1).astype(jnp.int32) * span + jnp.arange(n_assign, dtype=jnp.int32)
    sorted_assign = jnp.sort(keys) & (span - 1)
    r = jnp.arange(n_rows, dtype=jnp.int32)
    row_e = jnp.repeat(block_e, tm)
    within = r - start_pad[row_e]
    valid = within < counts[row_e]
    a = sorted_assign[jnp.clip(start_sorted[row_e] + within, 0, n_assign - 1)]
    spare0 = n_assign
    row_src = jnp.where(valid, a // TOP_K, 0)
    row_dst = jnp.where(valid, (a % TOP_K) * t + a // TOP_K, spare0 + (2 + (r // tm) % 2) * tm + r % tm)
    before_first = spare0 + tm + jnp.arange(tm, dtype=jnp.int32)
    dst_blocks = jnp.concatenate([before_first, row_dst]).reshape(n_blocks + 1, 1, tm)
    plan = dict(block_e=block_e.astype(jnp.int32), src_blocks=row_src.reshape(n_blocks, 1, tm) * n_sub,
                dst_blocks=dst_blocks * n_sub, spare0=spare0 * n_sub)
    return plan, n_assign + 4 * tm


def _combine_kernel(x2_ref, *refs):
    y_refs, (gate_ref, wn_ref, o_ref) = refs[:TOP_K], refs[TOP_K:]
    gate = gate_ref[...]
    y = x2_ref[...]
    tm, d = y.shape
    n_sub = d // LANES
    for k in range(TOP_K):
        rows = jnp.concatenate([y_refs[k][pl.ds(s, tm, stride=n_sub), :] for s in range(n_sub)], axis=1)
        y = y + rows * gate[:, k:k + 1]
    o_ref[...] = _rms(y, wn_ref[...])


def _combine(x2, y_rows, gate, wn, t_all, tok0):
    t, d = x2.shape
    tm = min(COMBINE_TILE, t)
    n_sub = d // LANES
    row = lambda i: (i, 0)
    y_spec = lambda k: pl.BlockSpec((tm * n_sub, LANES), lambda i: ((k * t_all + tok0) // tm + i, 0))
    return pl.pallas_call(
        _combine_kernel,
        grid=(t // tm,),
        in_specs=([pl.BlockSpec((tm, d), row)] + [y_spec(k) for k in range(TOP_K)]
                  + [pl.BlockSpec((tm, TOP_K), row), pl.BlockSpec((1, d), lambda i: (0, 0))]),
        out_specs=pl.BlockSpec((tm, d), row),
        out_shape=jax.ShapeDtypeStruct((t, d), F32),
        compiler_params=_params(1),
        name="moe_combine",
    )(x2, *([y_rows] * TOP_K), gate, wn.reshape(1, d))


def _mix(x2d, row0, nb, seq, pos, s0, attend, p, tok0, xn_all):
    t = nb * seq
    hw, dw = p["hw"], p["dw"]
    act_dtype = BF16 if seq % 16 == 0 else F32
    tm = min(ROW_TILE, t)
    tabs = _rope_tables(pos)
    if seq % tm == 0:
        tab_tiles = seq // tm
    else:
        tabs = tuple(jnp.tile(a, (nb, 1)) for a in tabs)
        tab_tiles = t // tm
    qh, f, vi, g, qd, k_rows, kb, v_rows, vb = _in_proj(
        x2d, row0, t, p["w_norm_mix"], p["w_in"], p["lb"], tabs, tab_tiles, act_dtype, hw, dw)
    o_h, s_fin = _hgrn(qh, f, vi, g, p["w_hgrn_norm"], s0, nb, seq, act_dtype)
    o_d = attend(qd, kb, vb, k_rows, v_rows, nb, seq)
    x2, xn_all, logits = _out_proj(x2d, row0, o_h, o_d, p["w_out"], p["w_norm_ffn"], p["w_router"], p["b_router"],
                                   tok0, xn_all)
    return dict(x2=x2, xn_all=xn_all, logits=logits, k_rows=k_rows, v_rows=v_rows, s_fin=s_fin, tok0=tok0)


def kernel(x_prompt, x_sample, cache_k, cache_v, state_hgrn, page_table, w_norm_mix, w_in, hgrn_lb_logits, w_hgrn_norm, diff_lambda_q1, diff_lambda_k1, diff_lambda_q2, diff_lambda_k2, w_subln, w_out, w_norm_ffn, w_router, b_router, w_gate, b_gate, w_up, b_up, w_down, b_down, w_norm_final):
    depth = w_in.shape[0]
    assert depth == 1, "single-layer trunk only"
    l = 0
    batch, seq, d = x_prompt.shape
    dec_batch, dec_seq, _ = x_sample.shape
    n_pool, page = cache_k.shape[1], cache_k.shape[2]
    past_len = page_table.shape[1] * page
    hw = w_hgrn_norm.shape[1]
    dw = (w_in.shape[2] - 4 * hw) // 3
    heads = dw // DIFF_V_DIM
    lower_bounds = jnp.cumsum(jax.nn.softmax(hgrn_lb_logits.astype(F32), axis=0), axis=0)
    lam_init = 0.8 - 0.6 * math.exp(-0.3 * l)
    lam = (jnp.exp(jnp.sum(diff_lambda_q1[l].astype(F32) * diff_lambda_k1[l].astype(F32)))
           - jnp.exp(jnp.sum(diff_lambda_q2[l].astype(F32) * diff_lambda_k2[l].astype(F32))) + lam_init)
    scal = jnp.stack([lam, jnp.asarray(1.0 - lam_init, F32)]).astype(F32)
    p = dict(hw=hw, dw=dw, lb=lower_bounds[l], w_norm_mix=w_norm_mix[l], w_in=w_in[l].astype(BF16),
             w_hgrn_norm=w_hgrn_norm[l], w_out=w_out[l].astype(BF16), w_norm_ffn=w_norm_ffn[l],
             w_router=w_router[l], b_router=b_router[l],
             w_gate=w_gate[l], b_gate=b_gate[l], w_up=w_up[l], b_up=b_up[l],
             w_down=w_down[l], b_down=b_down[l], w_norm_final=w_norm_final)
    wsub = w_subln[l]

    def attend_prompt(qd, kb, vb, k_rows, v_rows, nb, sq):
        return _causal_attn(qd, kb, vb, scal, wsub, nb, sq)

    cache_k2 = cache_k.reshape(-1, cache_k.shape[-1])
    cache_v2 = cache_v.reshape(-1, cache_v.shape[-1])
    page_ids = page_table.astype(jnp.int32) + l * n_pool

    def attend_sample(qd, kb, vb, k_rows, v_rows, nb, sq):
        return _paged_attn(qd, k_rows, v_rows, cache_k2, cache_v2, page_ids, page, scal, wsub, nb, sq)

    pos_prompt = jnp.arange(seq, dtype=jnp.int32)
    pos_sample = past_len + jnp.arange(dec_seq, dtype=jnp.int32)
    xp2d = x_prompt.reshape(batch * seq, d)
    xs2d = x_sample.reshape(dec_batch * dec_seq, d)

    t_p, t_s = batch * seq, dec_batch * dec_seq
    t_all = t_p + t_s
    xn_all = jnp.zeros((t_all * (d // LANES), LANES), F32)
    prm = _mix(xp2d, 0, batch, seq, pos_prompt, None, attend_prompt, p, 0, xn_all)
    smp = _mix(xs2d, 0, dec_batch, dec_seq, pos_sample, state_hgrn[l], attend_sample, p, t_p, prm["xn_all"])
    n_e = w_router.shape[2]
    gate, idx, counts = _router(jnp.concatenate([prm["logits"], smp["logits"]], axis=0))
    plan, n_out_rows = _plan_rows(idx, counts[0, :n_e], MOE_TILE, d // LANES)
    y_rows = _moe_ffn(smp["xn_all"], plan, n_out_rows, p["w_gate"], p["b_gate"], p["w_up"], p["b_up"],
                      p["w_down"], p["b_down"])
    y_p, y_s = (_combine(st["x2"], y_rows, lax.dynamic_slice_in_dim(gate, st["tok0"], st["x2"].shape[0]),
                         p["w_norm_final"], t_all, st["tok0"]) for st in (prm, smp))
    cache_rows = lambda a, nb, sq: a.reshape(1, nb, sq, heads, DIFF_V_DIM)
    return (y_p.reshape(batch, seq, d), y_s.reshape(dec_batch, dec_seq, d),
            cache_rows(prm["k_rows"], batch, seq), cache_rows(prm["v_rows"], batch, seq), prm["s_fin"][None],
            cache_rows(smp["k_rows"], dec_batch, dec_seq), cache_rows(smp["v_rows"], dec_batch, dec_seq),
            smp["s_fin"][None])
```

```python
import functools
import math

import jax
import jax.numpy as jnp
from jax import lax
from jax.experimental import pallas as pl
from jax.experimental.pallas import tpu as pltpu

F32 = jnp.float32
BF16 = jnp.bfloat16

LANES = 128
HGRN_HEAD_DIM = 128
HGRN_CHUNK = 64
DIFF_HEAD_DIM = 64
DIFF_V_DIM = 2 * DIFF_HEAD_DIM
ROT_DIM = DIFF_HEAD_DIM // 4
ROPE_THETA = 500000.0
TOP_K = 4
SWIGLU_LIMIT = 7.0
SWIGLU_ALPHA = 1.702
NORM_EPS = 1e-6
VMEM_LIMIT = 56 * 1024 * 1024

ROW_TILE = 512
ATTN_TILE = 1024
HGRN_TILE = 512
MOE_TILE = 256
PAGES_PER_STEP = 32
COMBINE_TILE = 256
ROUTER_TILE_MAX = 1024
ONES_ROWS = 16
QUERY_SCALE = DIFF_HEAD_DIM ** -0.5 * math.log2(math.e)

_NT = (((1,), (1,)), ((), ()))


def _params(n_axes):
    return pltpu.CompilerParams(dimension_semantics=("arbitrary",) * n_axes, vmem_limit_bytes=VMEM_LIMIT)


def _rms(x, w):
    return x * lax.rsqrt(jnp.mean(x * x, axis=-1, keepdims=True) + NORM_EPS) * w


def _in_proj_kernel(x_ref, wn_ref, w_ref, lb_ref, rc_ref, rp_ref, rm_ref,
                    qh_ref, f_ref, vi_ref, g_ref, qd_ref, kr_ref, kb_ref, vr_ref, vb_ref, *, hw, dw):
    h = _rms(x_ref[...], wn_ref[...]).astype(BF16)

    def proj(c0, width):
        return jnp.dot(h, w_ref[:, c0:c0 + width], preferred_element_type=F32)

    hq = proj(0, hw)
    qh_ref[...] = hq * jax.nn.sigmoid(hq)
    lb = lb_ref[...]
    f_ref[...] = lb + (1.0 - lb) * jax.nn.sigmoid(proj(hw, hw))
    vi_ref[...] = proj(2 * hw, hw)
    hg = proj(3 * hw, hw)
    g_ref[...] = hg * jax.nn.sigmoid(hg)

    rc, rp, rm = rc_ref[...], rp_ref[...], rm_ref[...]
    half = ROT_DIM // 2

    def rope(t):
        return t * rc + pltpu.roll(t, half, 1) * rp + pltpu.roll(t, LANES - half, 1) * rm

    dq = proj(4 * hw, dw)
    dk = proj(4 * hw + dw, dw)
    dv = proj(4 * hw + 2 * dw, dw)
    scale = QUERY_SCALE
    heads = dw // LANES
    tm = dq.shape[0]
    for c in range(heads):
        sl = slice(c * LANES, (c + 1) * LANES)
        qd_ref[:, sl] = (rope(dq[:, sl]) * scale).astype(qd_ref.dtype)
        kr = rope(dk[:, sl])
        kr_ref[pl.ds(c, tm, stride=heads), :] = kr
        vr_ref[pl.ds(c, tm, stride=heads), :] = dv[:, sl]
        kb_ref[:, sl] = kr.astype(kb_ref.dtype)
    vb_ref[...] = dv.astype(vb_ref.dtype)


def _in_proj(x2d, row0, t, wn, w_bf, lb, tabs, tab_tiles, act_dtype, hw, dw):
    d = x2d.shape[1]
    tm = min(ROW_TILE, t)
    heads = dw // LANES
    tile0 = row0 // tm
    row = lambda i: (i, 0)
    fixed = lambda i: (0, 0)
    tab = lambda i: (i % tab_tiles, 0)
    act = jax.ShapeDtypeStruct((t, dw), act_dtype)
    cache_rows = jax.ShapeDtypeStruct((t * heads, LANES), F32)
    out_shapes = [jax.ShapeDtypeStruct((t, hw), F32)] * 4 + [act, cache_rows, act, cache_rows, act]
    cache_spec = pl.BlockSpec((tm * heads, LANES), row)
    return pl.pallas_call(
        functools.partial(_in_proj_kernel, hw=hw, dw=dw),
        grid=(t // tm,),
        in_specs=[pl.BlockSpec((tm, d), lambda i: (tile0 + i, 0)),
                  pl.BlockSpec((1, d), fixed),
                  pl.BlockSpec(w_bf.shape, fixed, pipeline_mode=pl.Buffered(1)),
                  pl.BlockSpec((1, hw), fixed),
                  pl.BlockSpec((tm, LANES), tab), pl.BlockSpec((tm, LANES), tab), pl.BlockSpec((tm, LANES), tab)],
        out_specs=([pl.BlockSpec((tm, hw), row)] * 4
                   + [pl.BlockSpec((tm, dw), row), cache_spec, pl.BlockSpec((tm, dw), row), cache_spec,
                      pl.BlockSpec((tm, dw), row)]),
        out_shape=out_shapes,
        compiler_params=_params(1),
        name="in_proj",
    )(x2d, wn.reshape(1, d), w_bf, lb.reshape(1, hw), *tabs)


def _rope_tables(pos):
    half = ROT_DIM // 2
    inv = ROPE_THETA ** (-jnp.arange(0, ROT_DIM, 2, dtype=F32) / ROT_DIM)
    ang = pos.astype(F32)[:, None] * inv[None, :]
    cos, sin = jnp.cos(ang), jnp.sin(ang)
    n = pos.shape[0]
    rest = DIFF_HEAD_DIM - ROT_DIM
    zeros_h = jnp.zeros((n, half), F32)
    rc = jnp.concatenate([cos, cos, jnp.ones((n, rest), F32)], axis=1)
    rp = jnp.concatenate([zeros_h, sin, jnp.zeros((n, rest), F32)], axis=1)
    rm = jnp.concatenate([-sin, zeros_h, jnp.zeros((n, rest), F32)], axis=1)
    reps = LANES // DIFF_HEAD_DIM
    return tuple(jnp.tile(a, (1, reps)) for a in (rc, rp, rm))


def _hgrn_head(q, f, v, st):
    c_len = HGRN_CHUNK
    kdim = HGRN_HEAD_DIM
    rows = q.shape[0]
    if rows < c_len:
        pad = c_len - rows
        q = jnp.concatenate([q, jnp.zeros((pad, kdim), F32)], axis=0)
        v = jnp.concatenate([v, jnp.zeros((pad, kdim), F32)], axis=0)
        f = jnp.concatenate([f, jnp.ones((pad, kdim), F32)], axis=0)
    tl = q.shape[0]
    n_c = tl // c_len

    k = 1.0 - f
    b = jnp.log(f)
    r = lax.broadcasted_iota(jnp.int32, (tl, kdim), 0) % c_len
    s = 1
    while s < c_len:
        b = b + jnp.where(r >= s, pltpu.roll(b, s, 0), 0.0)
        s *= 2
    b3 = b.reshape(n_c, c_len, kdim)
    q3, k3, v3 = (a.reshape(n_c, c_len, kdim) for a in (q, k, v))
    bl = b3[:, c_len - 1:c_len, :]
    qe = (q3 * jnp.exp(b3)).astype(BF16)
    kd = (k3 * jnp.exp(-b3)).astype(BF16)
    kl = (k3 * jnp.exp(bl - b3)).astype(BF16)
    vb = v3.astype(BF16)
    dec = jnp.exp(bl)

    a = jnp.einsum('nck,ndk->ncd', qe, kd, preferred_element_type=F32)
    causal = (lax.broadcasted_iota(jnp.int32, (c_len, c_len), 1)
              <= lax.broadcasted_iota(jnp.int32, (c_len, c_len), 0))
    a = jnp.where(causal[None], a, 0.0).astype(BF16)
    o_intra = jnp.einsum('ncd,ndv->ncv', a, vb, preferred_element_type=F32)
    ds_t = jnp.einsum('ncv,nck->nvk', vb, kl, preferred_element_type=F32)

    outs = []
    for c in range(n_c):
        o_inter = lax.dot_general(qe[c], st.astype(BF16), _NT, preferred_element_type=F32)
        outs.append(o_intra[c] + o_inter)
        st = st * dec[c] + ds_t[c]
    o = outs[0] if n_c == 1 else jnp.concatenate(outs, axis=0)
    return o[:rows], st


def _hgrn_kernel(*refs, has_state):
    if has_state:
        q_ref, f_ref, v_ref, g_ref, wn_ref, s0_ref, o_ref, sfin_ref, st_ref = refs
    else:
        q_ref, f_ref, v_ref, g_ref, wn_ref, o_ref, sfin_ref, st_ref = refs
    j = pl.program_id(1)
    hd = HGRN_HEAD_DIM
    for h in range(st_ref.shape[0]):
        cols = slice(h * hd, (h + 1) * hd)

        @pl.when(j == 0)
        def _():
            st_ref[h] = s0_ref[0, h].T if has_state else jnp.zeros((hd, hd), F32)

        o, st = _hgrn_head(q_ref[:, cols], f_ref[:, cols], v_ref[:, cols], st_ref[h])
        st_ref[h] = st
        o_ref[:, cols] = (_rms(o, wn_ref[:, cols]) * g_ref[:, cols]).astype(o_ref.dtype)

        @pl.when(j == pl.num_programs(1) - 1)
        def _():
            sfin_ref[0, h] = st.T


def _hgrn(qh, f, vi, g, wn, s0, nb, seq, out_dtype):
    t, hw = qh.shape
    heads = hw // HGRN_HEAD_DIM
    tl = min(HGRN_TILE, seq)
    nl = seq // tl
    blk = pl.BlockSpec((tl, hw), lambda b, j: (b * nl + j, 0))
    state_blk = pl.BlockSpec((1, heads, HGRN_HEAD_DIM, HGRN_HEAD_DIM), lambda b, j: (b, 0, 0, 0))
    in_specs = [blk, blk, blk, blk, pl.BlockSpec((1, hw), lambda b, j: (0, 0))]
    args = [qh, f, vi, g, wn.reshape(1, hw)]
    if s0 is not None:
        in_specs.append(state_blk)
        args.append(s0)
    return pl.pallas_call(
        functools.partial(_hgrn_kernel, has_state=s0 is not None),
        grid=(nb, nl),
        in_specs=in_specs,
        out_specs=[blk, state_blk],
        out_shape=[jax.ShapeDtypeStruct((t, hw), out_dtype),
                   jax.ShapeDtypeStruct((nb, heads, HGRN_HEAD_DIM, HGRN_HEAD_DIM), F32)],
        scratch_shapes=[pltpu.VMEM((heads, HGRN_HEAD_DIM, HGRN_HEAD_DIM), F32)],
        compiler_params=_params(2),
        name="hgrn",
    )(*args)


def _split_components(q):
    lane = lax.broadcasted_iota(jnp.int32, q.shape, 1)
    zero = jnp.zeros_like(q)
    return jnp.where(lane < DIFF_HEAD_DIM, q, zero), jnp.where(lane >= DIFF_HEAD_DIM, q, zero)


def _online_update(s, vb, m, l, acc):
    m_new = jnp.maximum(m, jnp.max(s, axis=-1, keepdims=True))
    alpha = jnp.exp2(m - m_new)
    p = jnp.exp2(s - m_new)
    l = alpha * l + jnp.sum(p, axis=-1, keepdims=True)
    acc = alpha * acc + jnp.dot(p.astype(BF16), vb, preferred_element_type=F32)
    return m_new, l, acc


def _diff_finish(o0, o1, sc_ref, wsub_ref):
    o = o0 - sc_ref[0] * o1
    return _rms(o, wsub_ref[...]) * sc_ref[1]


def _causal_attn_kernel(sc_ref, q_ref, k_ref, vt_ref, wsub_ref, o_ref, s_a, s_b, *, tile):
    i = pl.program_id(2)
    q_parts = _split_components(q_ref[...])
    vd = DIFF_V_DIM
    kt = tile // 2

    def scores_to(step, buf):
        kb = k_ref[pl.ds(pl.multiple_of(step * kt, kt), kt), :]
        for c in range(2):
            buf[c] = lax.dot_general(kb, q_parts[c], _NT, preferred_element_type=F32)

    def absorb(step, buf, stats, diagonal_half):
        vt = vt_ref[0, 0, step]
        out = []
        for c in range(2):
            m, acc = stats[2 * c], stats[2 * c + 1]
            sc = buf[c]
            if diagonal_half is not None:
                keep = (lax.broadcasted_iota(jnp.int32, (kt, tile), 0) + diagonal_half * kt
                        <= lax.broadcasted_iota(jnp.int32, (kt, tile), 1))
                sc = jnp.where(keep, sc, -jnp.inf)
            m_new = jnp.maximum(m, jnp.max(sc, axis=0, keepdims=True))
            p = jnp.exp2(sc - m_new).astype(BF16)
            acc = jnp.exp2(m - m_new) * acc + jnp.dot(vt, p, preferred_element_type=F32)
            out += [m_new, acc]
        return tuple(out)

    def body(j, stats):
        scores_to(2 * j + 1, s_b)
        stats = absorb(2 * j, s_a, stats, None)
        scores_to(2 * j + 2, s_a)
        return absorb(2 * j + 1, s_b, stats, None)

    neg = jnp.full((1, tile), -jnp.inf, F32)
    zero_acc = jnp.zeros((vd + ONES_ROWS, tile), F32)
    scores_to(0, s_a)
    stats = lax.fori_loop(0, i, body, (neg, zero_acc, neg, zero_acc))
    scores_to(2 * i + 1, s_b)
    stats = absorb(2 * i, s_a, stats, 0)
    _, a0, _, a1 = absorb(2 * i + 1, s_b, stats, 1)
    o = a0[:vd] / a0[vd:vd + 1] - sc_ref[0] * (a1[:vd] / a1[vd:vd + 1])
    o = o * lax.rsqrt(jnp.mean(o * o, axis=0, keepdims=True) + NORM_EPS) * wsub_ref[...] * sc_ref[1]
    o_ref[...] = o.T.astype(o_ref.dtype)


def _causal_attn(qd, kb, vb, scal, wsub, nb, seq):
    t, dw = qd.shape
    heads = dw // DIFF_V_DIM
    tile = min(ATTN_TILE, seq)
    nq = seq // tile
    kt = tile // 2
    nk = seq // kt
    vt = vb.reshape(nb, nk, kt, heads, DIFF_V_DIM).transpose(0, 3, 1, 4, 2)
    vt = jnp.concatenate([vt, jnp.ones((nb, heads, nk, ONES_ROWS, kt), vt.dtype)], axis=3)
    return pl.pallas_call(
        functools.partial(_causal_attn_kernel, tile=tile),
        grid=(nb, heads, nq),
        in_specs=[pl.BlockSpec(memory_space=pltpu.SMEM),
                  pl.BlockSpec((tile, DIFF_V_DIM), lambda b, h, i: (b * nq + i, h)),
                  pl.BlockSpec((seq, DIFF_V_DIM), lambda b, h, i: (b, h)),
                  pl.BlockSpec((1, 1, nk, DIFF_V_DIM + ONES_ROWS, kt), lambda b, h, i: (b, h, 0, 0, 0)),
                  pl.BlockSpec((DIFF_V_DIM, 1), lambda b, h, i: (0, 0))],
        out_specs=pl.BlockSpec((tile, DIFF_V_DIM), lambda b, h, i: (b * nq + i, h)),
        out_shape=jax.ShapeDtypeStruct((t, dw), BF16),
        scratch_shapes=[pltpu.VMEM((2, kt, tile), F32), pltpu.VMEM((2, kt, tile), F32)],
        compiler_params=_params(3),
        name="causal_diff_attn",
    )(scal, qd, kb, vt, wsub.reshape(DIFF_V_DIM, 1))


def _paged_attn_kernel(*refs, heads, dec_seq, n_pages):
    pt_ref, sc_ref, q_ref = refs[0], refs[1], refs[2]
    k_pages = refs[3:3 + n_pages]
    v_pages = refs[3 + n_pages:3 + 2 * n_pages]
    kn_ref, vn_ref, wsub_ref, o_ref, m_ref, l_ref, acc_ref = refs[3 + 2 * n_pages:]
    del pt_ref
    g = pl.program_id(1)
    per_head = 2 * dec_seq
    rows = heads * per_head

    @pl.when(g == 0)
    def _():
        m_ref[...] = jnp.full(m_ref.shape, -jnp.inf, F32)
        l_ref[...] = jnp.zeros(l_ref.shape, F32)
        acc_ref[...] = jnp.zeros(acc_ref.shape, F32)

    q = q_ref[...].astype(BF16)
    parts = []
    for h in range(heads):
        parts.extend(_split_components(q[:, h * DIFF_V_DIM:(h + 1) * DIFF_V_DIM]))
    q_all = jnp.concatenate(parts, axis=0)

    def accumulate(kb, vb, keep):
        s = lax.dot_general(q_all, kb, _NT, preferred_element_type=F32)
        m, l, acc = _online_update(jnp.where(keep, s, -jnp.inf), vb, m_ref[...], l_ref[...], acc_ref[...])
        m_ref[...], l_ref[...], acc_ref[...] = m, l, acc

    n_keys = n_pages * k_pages[0].shape[0]
    col = lax.broadcasted_iota(jnp.int32, (rows, n_keys), 1)
    row = lax.broadcasted_iota(jnp.int32, (rows, n_keys), 0)
    accumulate(jnp.concatenate([kp[...].astype(BF16) for kp in k_pages], axis=0),
               jnp.concatenate([vp[...].astype(BF16) for vp in v_pages], axis=0),
               col % heads == row // per_head)

    @pl.when(g == pl.num_programs(1) - 1)
    def _():
        n_new = kn_ref.shape[1]
        col = lax.broadcasted_iota(jnp.int32, (rows, n_new), 1)
        row = lax.broadcasted_iota(jnp.int32, (rows, n_new), 0)
        keep = (col % heads == row // per_head) & (col // heads <= row % dec_seq)
        accumulate(kn_ref[0].astype(BF16), vn_ref[0].astype(BF16), keep)
        o = acc_ref[...] / l_ref[...]
        for h in range(heads):
            o0 = o[h * per_head:h * per_head + dec_seq]
            o1 = o[h * per_head + dec_seq:(h + 1) * per_head]
            o_ref[:, h * DIFF_V_DIM:(h + 1) * DIFF_V_DIM] = _diff_finish(o0, o1, sc_ref, wsub_ref).astype(o_ref.dtype)


def _paged_attn(qd, k_new, v_new, cache_k2, cache_v2, page_ids, page, scal, wsub, nb, dec_seq):
    t, dw = qd.shape
    heads = dw // DIFF_V_DIM
    page_rows = page * heads
    n_pages = page_ids.shape[1]
    gp = min(PAGES_PER_STEP, n_pages)
    n_steps = n_pages // gp
    new_rows = -(-dec_seq * heads // LANES) * LANES
    pad = lambda a: jnp.pad(a.reshape(nb, dec_seq * heads, DIFF_V_DIM), ((0, 0), (0, new_rows - dec_seq * heads), (0, 0)))
    page_spec = lambda i: pl.BlockSpec((page_rows, DIFF_V_DIM), lambda b, g, pt: (pt[b, g * gp + i], 0))
    tok_spec = pl.BlockSpec((dec_seq, dw), lambda b, g, pt: (b, 0))
    new_spec = pl.BlockSpec((1, new_rows, DIFF_V_DIM), lambda b, g, pt: (b, 0, 0))
    rows = heads * 2 * dec_seq
    grid_spec = pltpu.PrefetchScalarGridSpec(
        num_scalar_prefetch=1,
        grid=(nb, n_steps),
        in_specs=([pl.BlockSpec(memory_space=pltpu.SMEM), tok_spec]
                  + [page_spec(i) for i in range(gp)] * 2
                  + [new_spec, new_spec, pl.BlockSpec((1, DIFF_V_DIM), lambda b, g, pt: (0, 0))]),
        out_specs=tok_spec,
        scratch_shapes=[pltpu.VMEM((rows, 1), F32), pltpu.VMEM((rows, 1), F32), pltpu.VMEM((rows, DIFF_V_DIM), F32)],
    )
    return pl.pallas_call(
        functools.partial(_paged_attn_kernel, heads=heads, dec_seq=dec_seq, n_pages=gp),
        grid_spec=grid_spec,
        out_shape=jax.ShapeDtypeStruct((t, dw), F32),
        compiler_params=_params(2),
        name="paged_diff_attn",
    )(page_ids, scal, qd, *([cache_k2] * gp), *([cache_v2] * gp), pad(k_new), pad(v_new),
      wsub.reshape(1, DIFF_V_DIM))


def _out_proj_kernel(x_ref, oh_ref, od_ref, wo_ref, wn_ref, wr_hi_ref, wr_lo_ref, br_ref, xn_all_ref,
                     x2_ref, xn_ref, lg_ref, *, hw):
    del xn_all_ref
    mix = (jnp.dot(oh_ref[...].astype(BF16), wo_ref[:hw, :], preferred_element_type=F32)
           + jnp.dot(od_ref[...].astype(BF16), wo_ref[hw:, :], preferred_element_type=F32))
    x2 = x_ref[...] + mix
    x2_ref[...] = x2
    xn = _rms(x2, wn_ref[...])
    n_sub = xn.shape[1] // LANES
    for s in range(n_sub):
        xn_ref[pl.ds(s, xn.shape[0], stride=n_sub), :] = xn[:, s * LANES:(s + 1) * LANES]
    xn_hi = xn.astype(BF16)
    xn_lo = (xn - xn_hi.astype(F32)).astype(BF16)
    lg_ref[...] = (jnp.dot(xn_hi, wr_hi_ref[...], preferred_element_type=F32)
                   + jnp.dot(xn_lo, wr_hi_ref[...], preferred_element_type=F32)
                   + jnp.dot(xn_hi, wr_lo_ref[...], preferred_element_type=F32)
                   + br_ref[...])


def _out_proj(x2d, row0, o_h, o_d, wo_bf, wn, w_router, b_router, tok0, xn_all):
    d = x2d.shape[1]
    t, hw = o_h.shape
    dw = o_d.shape[1]
    n_e = w_router.shape[1]
    tm = min(ROW_TILE, t)
    tile0 = row0 // tm
    xn_tile0 = tok0 // tm
    wr = jnp.pad(w_router.astype(F32), ((0, 0), (0, LANES - n_e)))
    wr_hi = wr.astype(BF16)
    wr_lo = (wr - wr_hi.astype(F32)).astype(BF16)
    br = jnp.pad(b_router.astype(F32), (0, LANES - n_e), constant_values=-jnp.inf).reshape(1, LANES)
    row = lambda i: (i, 0)
    fixed = lambda i: (0, 0)
    n_sub = d // LANES
    return pl.pallas_call(
        functools.partial(_out_proj_kernel, hw=hw),
        grid=(t // tm,),
        in_specs=[pl.BlockSpec((tm, d), lambda i: (tile0 + i, 0)), pl.BlockSpec((tm, hw), row),
                  pl.BlockSpec((tm, dw), row),
                  pl.BlockSpec(wo_bf.shape, fixed), pl.BlockSpec((1, d), fixed),
                  pl.BlockSpec((d, LANES), fixed), pl.BlockSpec((d, LANES), fixed), pl.BlockSpec((1, LANES), fixed),
                  pl.BlockSpec(memory_space=pl.ANY)],
        out_specs=[pl.BlockSpec((tm, d), row), pl.BlockSpec((tm * n_sub, LANES), lambda i: (xn_tile0 + i, 0)),
                   pl.BlockSpec((tm, LANES), row)],
        out_shape=[jax.ShapeDtypeStruct((t, d), F32), jax.ShapeDtypeStruct(xn_all.shape, F32),
                   jax.ShapeDtypeStruct((t, LANES), F32)],
        input_output_aliases={8: 1},
        compiler_params=_params(1),
        name="out_proj_router",
    )(x2d, o_h, o_d, wo_bf, wn.reshape(1, d), wr_hi, wr_lo, br, xn_all)


def _router_kernel(lg_ref, gate_ref, idx_ref, cnt_ref, base_ref):
    i = pl.program_id(0)

    @pl.when(i == 0)
    def _():
        base_ref[...] = jnp.zeros(base_ref.shape, F32)

    cur = lg_ref[...]
    tm = cur.shape[0]
    lane = lax.broadcasted_iota(jnp.int32, cur.shape, 1)
    vals, idxs, hits = [], [], []
    for _ in range(TOP_K):
        mx = jnp.max(cur, axis=-1, keepdims=True)
        ix = jnp.min(jnp.where(cur == mx, lane, LANES), axis=-1, keepdims=True)
        hit = lane == ix
        cur = jnp.where(hit, -jnp.inf, cur)
        vals.append(mx)
        idxs.append(ix)
        hits.append(hit)
    es = [jnp.exp(v - vals[0]) for v in vals]
    denom = functools.reduce(lambda a, b: a + b, es)

    chosen = functools.reduce(lambda a, b: a | b, hits)
    chosen_f = jnp.where(chosen, 1.0, 0.0)
    lane_k = lax.broadcasted_iota(jnp.int32, (tm, TOP_K), 1)
    gate = jnp.zeros((tm, TOP_K), F32)
    idx = jnp.zeros((tm, TOP_K), jnp.int32)
    for k in range(TOP_K):
        gate = jnp.where(lane_k == k, es[k] / denom, gate)
        idx = jnp.where(lane_k == k, idxs[k], idx)
    gate_ref[...] = gate
    idx_ref[...] = idx
    total = base_ref[...] + jnp.sum(chosen_f, axis=0, keepdims=True)
    base_ref[...] = total
    cnt_ref[...] = total.astype(jnp.int32)


def _router(logits):
    t = logits.shape[0]
    tm = max(c for c in range(8, ROUTER_TILE_MAX + 1, 8) if t % c == 0)
    row = lambda i: (i, 0)
    return pl.pallas_call(
        _router_kernel,
        grid=(t // tm,),
        in_specs=[pl.BlockSpec((tm, LANES), row)],
        out_specs=[pl.BlockSpec((tm, TOP_K), row)] * 2 + [pl.BlockSpec((1, LANES), lambda i: (0, 0))],
        out_shape=[jax.ShapeDtypeStruct((t, TOP_K), F32), jax.ShapeDtypeStruct((t, TOP_K), jnp.int32),
                   jax.ShapeDtypeStruct((1, LANES), jnp.int32)],
        scratch_shapes=[pltpu.VMEM((1, LANES), F32)],
        compiler_params=_params(1),
        name="router_topk",
    )(logits)


def _moe_ffn_kernel(be_ref, src0_ref, src1_ref, src_ref, dstp_ref, dstl_ref, x_hbm, wg_ref, bg_ref, wu_ref, bu_ref,
                    wd_ref, bd_ref, y_hbm, wg_bf, wu_bf, wd_bf, xbuf, xs, ybuf, sem_g, sem_s, *, tm, spare0):
    i = pl.program_id(0)
    last = pl.num_programs(0) - 1
    cur = lax.rem(i, 2)
    prev = 1 - cur
    xcur = lax.rem(i, 3)
    xahead = lax.rem(i + 2, 3)
    n_sub = xbuf.shape[1] // tm

    def token_copy(idx_ref, r, slot, to_vmem):
        hbm_rows = pl.ds(pl.multiple_of(idx_ref[0, 0, r], n_sub), n_sub)
        vmem_rows = pl.ds(r * n_sub, n_sub)
        if to_vmem:
            return pltpu.make_async_copy(x_hbm.at[hbm_rows], xbuf.at[slot, vmem_rows], sem_g.at[slot])
        return pltpu.make_async_copy(ybuf.at[slot, vmem_rows], y_hbm.at[hbm_rows], sem_s.at[slot])

    def gather_rows(idx_ref, slot):
        for r in range(tm):
            token_copy(idx_ref, r, slot, True).start(priority=r % 2)

    def scatter_rows(idx_ref, slot):
        for r in range(tm):
            token_copy(idx_ref, r, slot, False).start(priority=1)

    def wait_gather(slot):
        pltpu.make_async_copy(x_hbm.at[pl.ds(0, tm * n_sub)], xbuf.at[slot], sem_g.at[slot]).wait()

    def wait_scatter(slot):
        pltpu.make_async_copy(ybuf.at[slot], y_hbm.at[pl.ds(0, tm * n_sub)], sem_s.at[slot]).wait()

    @pl.when(i == 0)
    def _():
        gather_rows(src0_ref, 0)
        gather_rows(src1_ref, 1)
        ybuf[...] = jnp.zeros(ybuf.shape, F32)
        pltpu.make_async_copy(ybuf.at[0], y_hbm.at[pl.ds(spare0, tm * n_sub)], sem_s.at[0]).start()

    @pl.when((i == 0) | (be_ref[i] != be_ref[jnp.maximum(i - 1, 0)]))
    def _():
        wg_bf[...] = wg_ref[0].astype(BF16)
        wu_bf[...] = wu_ref[0].astype(BF16)
        wd_bf[...] = wd_ref[0].astype(BF16)

    wait_gather(xcur)
    for s in range(n_sub):
        xs[:, s * LANES:(s + 1) * LANES] = xbuf[xcur, pl.ds(s, tm, stride=n_sub), :].astype(BF16)
    gather_rows(src_ref, xahead)
    scatter_rows(dstp_ref, prev)
    x = xs[...]
    cw = 2 * LANES
    h_parts = []
    for c in range(wg_bf.shape[1] // cw):
        cols = slice(c * cw, (c + 1) * cw)
        g = jnp.dot(x, wg_bf[:, cols], preferred_element_type=F32) + bg_ref[0, :, cols]
        u = jnp.dot(x, wu_bf[:, cols], preferred_element_type=F32) + bu_ref[0, :, cols]
        g = jnp.minimum(g, SWIGLU_LIMIT)
        u = jnp.clip(u, -SWIGLU_LIMIT, SWIGLU_LIMIT)
        h_parts.append(((u + 1.0) * (g * jax.nn.sigmoid(SWIGLU_ALPHA * g))).astype(BF16))
    h = jnp.concatenate(h_parts, axis=1)
    wait_scatter(cur)
    for c in range(n_sub * LANES // cw):
        yc = jnp.dot(h, wd_bf[:, c * cw:(c + 1) * cw], preferred_element_type=F32) + bd_ref[0, :, c * cw:(c + 1) * cw]
        for j in range(cw // LANES):
            s = c * (cw // LANES) + j
            ybuf[cur, pl.ds(s, tm, stride=n_sub), :] = yc[:, j * LANES:(j + 1) * LANES]

    @pl.when(i == last)
    def _():
        scatter_rows(dstl_ref, cur)
        wait_scatter(cur)
        wait_scatter(prev)
        wait_gather(lax.rem(i + 1, 3))
        wait_gather(xahead)


def _moe_ffn(xn_rows, plan, n_out_rows, wg, bg, wu, bu, wd, bd):
    d = wg.shape[1]
    n_sub = d // LANES
    n_e, _, d_ff = wg.shape
    tm = MOE_TILE
    block_e, src_blocks, dst_blocks = plan["block_e"], plan["src_blocks"], plan["dst_blocks"]
    n_blocks = block_e.shape[0]
    w_spec = lambda k, n: pl.BlockSpec((1, k, n), lambda i, be: (be[i], 0, 0))
    idx_spec = lambda f: pl.BlockSpec((1, 1, tm), lambda i, be: (f(i), 0, 0), memory_space=pltpu.SMEM)
    grid_spec = pltpu.PrefetchScalarGridSpec(
        num_scalar_prefetch=1,
        grid=(n_blocks,),
        in_specs=[idx_spec(lambda i: 0), idx_spec(lambda i: min(1, n_blocks - 1)),
                  idx_spec(lambda i: jnp.minimum(i + 2, n_blocks - 1)),
                  idx_spec(lambda i: i), idx_spec(lambda i: n_blocks),
                  pl.BlockSpec(memory_space=pl.ANY),
                  w_spec(d, d_ff), w_spec(1, d_ff), w_spec(d, d_ff), w_spec(1, d_ff), w_spec(d_ff, d), w_spec(1, d)],
        out_specs=pl.BlockSpec(memory_space=pl.ANY),
        scratch_shapes=[pltpu.VMEM((d, d_ff), BF16), pltpu.VMEM((d, d_ff), BF16), pltpu.VMEM((d_ff, d), BF16),
                        pltpu.VMEM((3, tm * n_sub, LANES), F32), pltpu.VMEM((tm, d), BF16),
                        pltpu.VMEM((2, tm * n_sub, LANES), F32),
                        pltpu.SemaphoreType.DMA((3,)), pltpu.SemaphoreType.DMA((2,))],
    )
    return pl.pallas_call(
        functools.partial(_moe_ffn_kernel, tm=tm, spare0=plan["spare0"]),
        grid_spec=grid_spec,
        out_shape=jax.ShapeDtypeStruct((n_out_rows * n_sub, LANES), F32),
        compiler_params=_params(1),
        name="moe_ffn",
    )(block_e, src_blocks, src_blocks, src_blocks, dst_blocks, dst_blocks, xn_rows,
      wg, bg.reshape(n_e, 1, d_ff), wu, bu.reshape(n_e, 1, d_ff), wd, bd.reshape(n_e, 1, d))


def _plan_rows(idx, counts, tm, n_sub):
    t = idx.shape[0]
    n_e = counts.shape[0]
    n_assign = t * TOP_K
    padded = (counts + tm - 1) // tm * tm
    pad_end = jnp.cumsum(padded)
    start_pad = pad_end - padded
    start_sorted = jnp.cumsum(counts) - counts
    n_rows = (n_assign + n_e * (tm - 1) + tm - 1) // tm * tm
    n_blocks = n_rows // tm
    block_start = jnp.arange(n_blocks, dtype=jnp.int32) * tm
    block_e = jnp.minimum(jnp.sum((pad_end[None, :] <= block_start[:, None]).astype(jnp.int32), axis=1), n_e - 1)
    span = 1 << (n_assign - 1).bit_length()
    assert n_e * span < 2 ** 31
    keys = idx.reshape(-1).astype(jnp.int32) * span + jnp.arange(n_assign, dtype=jnp.int32)
    sorted_assign = jnp.sort(keys) & (span - 1)
    r = jnp.arange(n_rows, dtype=jnp.int32)
    row_e = jnp.repeat(block_e, tm)
    within = r - start_pad[row_e]
    valid = within < counts[row_e]
    a = sorted_assign[jnp.clip(start_sorted[row_e] + within, 0, n_assign - 1)]
    spare0 = n_assign
    row_src = jnp.where(valid, a // TOP_K, 0)
    row_dst = jnp.where(valid, (a % TOP_K) * t + a // TOP_K, spare0 + (2 + (r // tm) % 2) * tm + r % tm)
    before_first = spare0 + tm + jnp.arange(tm, dtype=jnp.int32)
    dst_blocks = jnp.concatenate([before_first, row_dst]).reshape(n_blocks + 1, 1, tm)
    plan = dict(block_e=block_e.astype(jnp.int32), src_blocks=row_src.reshape(n_blocks, 1, tm) * n_sub,
                dst_blocks=dst_blocks * n_sub, spare0=spare0 * n_sub)
    return plan, n_assign + 4 * tm


def _combine_kernel(x2_ref, *refs):
    y_refs, (gate_ref, wn_ref, o_ref) = refs[:TOP_K], refs[TOP_K:]
    gate = gate_ref[...]
    y = x2_ref[...]
    tm, d = y.shape
    n_sub = d // LANES
    for k in range(TOP_K):
        rows = jnp.concatenate([y_refs[k][pl.ds(s, tm, stride=n_sub), :] for s in range(n_sub)], axis=1)
        y = y + rows * gate[:, k:k + 1]
    o_ref[...] = _rms(y, wn_ref[...])


def _combine(x2, y_rows, gate, wn, t_all, tok0):
    t, d = x2.shape
    tm = min(COMBINE_TILE, t)
    n_sub = d // LANES
    row = lambda i: (i, 0)
    y_spec = lambda k: pl.BlockSpec((tm * n_sub, LANES), lambda i: ((k * t_all + tok0) // tm + i, 0))
    return pl.pallas_call(
        _combine_kernel,
        grid=(t // tm,),
        in_specs=([pl.BlockSpec((tm, d), row)] + [y_spec(k) for k in range(TOP_K)]
                  + [pl.BlockSpec((tm, TOP_K), row), pl.BlockSpec((1, d), lambda i: (0, 0))]),
        out_specs=pl.BlockSpec((tm, d), row),
        out_shape=jax.ShapeDtypeStruct((t, d), F32),
        compiler_params=_params(1),
        name="moe_combine",
    )(x2, *([y_rows] * TOP_K), gate, wn.reshape(1, d))


def _mix(x2d, row0, nb, seq, pos, s0, attend, p, tok0, xn_all):
    t = nb * seq
    hw, dw = p["hw"], p["dw"]
    act_dtype = BF16 if seq % 16 == 0 else F32
    tm = min(ROW_TILE, t)
    tabs = _rope_tables(pos)
    if seq % tm == 0:
        tab_tiles = seq // tm
    else:
        tabs = tuple(jnp.tile(a, (nb, 1)) for a in tabs)
        tab_tiles = t // tm
    qh, f, vi, g, qd, k_rows, kb, v_rows, vb = _in_proj(
        x2d, row0, t, p["w_norm_mix"], p["w_in"], p["lb"], tabs, tab_tiles, act_dtype, hw, dw)
    o_h, s_fin = _hgrn(qh, f, vi, g, p["w_hgrn_norm"], s0, nb, seq, act_dtype)
    o_d = attend(qd, kb, vb, k_rows, v_rows, nb, seq)
    x2, xn_all, logits = _out_proj(x2d, row0, o_h, o_d, p["w_out"], p["w_norm_ffn"], p["w_router"], p["b_router"],
                                   tok0, xn_all)
    return dict(x2=x2, xn_all=xn_all, logits=logits, k_rows=k_rows, v_rows=v_rows, s_fin=s_fin, tok0=tok0)


def kernel(x_prompt, x_sample, cache_k, cache_v, state_hgrn, page_table, w_norm_mix, w_in, hgrn_lb_logits, w_hgrn_norm, diff_lambda_q1, diff_lambda_k1, diff_lambda_q2, diff_lambda_k2, w_subln, w_out, w_norm_ffn, w_router, b_router, w_gate, b_gate, w_up, b_up, w_down, b_down, w_norm_final):
    depth = w_in.shape[0]
    assert depth == 1, "single-layer trunk only"
    l = 0
    batch, seq, d = x_prompt.shape
    dec_batch, dec_seq, _ = x_sample.shape
    n_pool, page = cache_k.shape[1], cache_k.shape[2]
    past_len = page_table.shape[1] * page
    hw = w_hgrn_norm.shape[1]
    dw = (w_in.shape[2] - 4 * hw) // 3
    heads = dw // DIFF_V_DIM
    lower_bounds = jnp.cumsum(jax.nn.softmax(hgrn_lb_logits.astype(F32), axis=0), axis=0)
    lam_init = 0.8 - 0.6 * math.exp(-0.3 * l)
    lam = (jnp.exp(jnp.sum(diff_lambda_q1[l].astype(F32) * diff_lambda_k1[l].astype(F32)))
           - jnp.exp(jnp.sum(diff_lambda_q2[l].astype(F32) * diff_lambda_k2[l].astype(F32))) + lam_init)
    scal = jnp.stack([lam, jnp.asarray(1.0 - lam_init, F32)]).astype(F32)
    p = dict(hw=hw, dw=dw, lb=lower_bounds[l], w_norm_mix=w_norm_mix[l], w_in=w_in[l].astype(BF16),
             w_hgrn_norm=w_hgrn_norm[l], w_out=w_out[l].astype(BF16), w_norm_ffn=w_norm_ffn[l],
             w_router=w_router[l], b_router=b_router[l],
             w_gate=w_gate[l], b_gate=b_gate[l], w_up=w_up[l], b_up=b_up[l],
             w_down=w_down[l], b_down=b_down[l], w_norm_final=w_norm_final)
    wsub = w_subln[l]

    def attend_prompt(qd, kb, vb, k_rows, v_rows, nb, sq):
        return _causal_attn(qd, kb, vb, scal, wsub, nb, sq)

    cache_k2 = cache_k.reshape(-1, cache_k.shape[-1])
    cache_v2 = cache_v.reshape(-1, cache_v.shape[-1])
    page_ids = page_table.astype(jnp.int32) + l * n_pool

    def attend_sample(qd, kb, vb, k_rows, v_rows, nb, sq):
        return _paged_attn(qd, k_rows, v_rows, cache_k2, cache_v2, page_ids, page, scal, wsub, nb, sq)

    pos_prompt = jnp.arange(seq, dtype=jnp.int32)
    pos_sample = past_len + jnp.arange(dec_seq, dtype=jnp.int32)
    xp2d = x_prompt.reshape(batch * seq, d)
    xs2d = x_sample.reshape(dec_batch * dec_seq, d)

    t_p, t_s = batch * seq, dec_batch * dec_seq
    t_all = t_p + t_s
    xn_all = jnp.zeros((t_all * (d // LANES), LANES), F32)
    prm = _mix(xp2d, 0, batch, seq, pos_prompt, None, attend_prompt, p, 0, xn_all)
    smp = _mix(xs2d, 0, dec_batch, dec_seq, pos_sample, state_hgrn[l], attend_sample, p, t_p, prm["xn_all"])
    n_e = w_router.shape[2]
    gate, idx, counts = _router(jnp.concatenate([prm["logits"], smp["logits"]], axis=0))
    plan, n_out_rows = _plan_rows(idx, counts[0, :n_e], MOE_TILE, d // LANES)
    y_rows = _moe_ffn(smp["xn_all"], plan, n_out_rows, p["w_gate"], p["b_gate"], p["w_up"], p["b_up"],
                      p["w_down"], p["b_down"])
    y_p, y_s = (_combine(st["x2"], y_rows, lax.dynamic_slice_in_dim(gate, st["tok0"], st["x2"].shape[0]),
                         p["w_norm_final"], t_all, st["tok0"]) for st in (prm, smp))
    cache_rows = lambda a, nb, sq: a.reshape(1, nb, sq, heads, DIFF_V_DIM)
    return (y_p.reshape(batch, seq, d), y_s.reshape(dec_batch, dec_seq, d),
            cache_rows(prm["k_rows"], batch, seq), cache_rows(prm["v_rows"], batch, seq), prm["s_fin"][None],
            cache_rows(smp["k_rows"], dec_batch, dec_seq), cache_rows(smp["v_rows"], dec_batch, dec_seq),
            smp["s_fin"][None])
```
